```python
import math
import jax, jax.numpy as jnp
from jax import lax
import numpy as np

D_MODEL = 1024
BATCH = 2
SEQ = 8192
DEPTH = 4
DEC_BATCH = 128
DEC_SEQ = 8
PAST_LEN = 8192
PAGE_SIZE = 128

HEAD_DIM = 64
N_HEADS = D_MODEL // HEAD_DIM
ATTN_SCALE = HEAD_DIM ** -0.5
ROPE_THETA = 10000.0
RMS_EPS = 1e-6
Q_BLOCK = 128
N_MIXERS = 3
N_LAYERS_A = (DEPTH + 2) // 3
N_LAYERS_B = (DEPTH + 1) // 3
N_LAYERS_C = DEPTH // 3
NSA_KV = 2
NSA_HPG = N_HEADS // NSA_KV
NSA_CMP = 64
NSA_SLC = 64
NSA_TOPK = 16
NSA_WIN = 512
NSA_FORCE = 1000.0
SWA_KV = 2
SWA_WIN = 128
FOX_KV = 4
FOX_GATE_BIAS = 3.0
D_FF = 2816
CONV_W = 3

kernel_name = "hybrid_nsa_swa_fox_convffn_step"


def rms_norm(x, g):
    xf = x.astype(jnp.float32)
    y = xf * lax.rsqrt(jnp.mean(xf * xf, axis=-1, keepdims=True) + RMS_EPS)
    return (y * g.astype(jnp.float32)).astype(x.dtype)


def rope(x, pos):
    half = HEAD_DIM // 2
    inv_freq = jnp.exp(-math.log(ROPE_THETA) * jnp.arange(half, dtype=jnp.float32) / half)
    ang = pos.astype(jnp.float32)[:, None] * inv_freq[None, :]
    cos = jnp.cos(ang)[:, None, :]
    sin = jnp.sin(ang)[:, None, :]
    xf = x.astype(jnp.float32)
    x1, x2 = xf[..., :half], xf[..., half:]
    return jnp.concatenate([x1 * cos - x2 * sin, x2 * cos + x1 * sin], axis=-1).astype(x.dtype)


def rope_key_slots(kv, pos):
    b, t, s, g, d = kv.shape
    rot = rope(kv.reshape(b, t, s * g, d), pos).reshape(kv.shape)
    is_key = (jnp.arange(s) % 2 == 0)[:, None, None]
    return jnp.where(is_key, rot, kv)


def masked_softmax(s, mask):
    s = jnp.where(mask, s.astype(jnp.float32), -jnp.inf)
    m = jnp.max(s, axis=-1, keepdims=True)
    e = jnp.exp(s - jnp.where(jnp.isfinite(m), m, 0.0))
    return e / jnp.maximum(jnp.sum(e, axis=-1, keepdims=True), 1e-30)


def attend(q, k, v, mask, bias=None, sink=None):
    n_q, n_kv = q.shape[0], k.shape[1]
    qg = q.reshape(n_q, n_kv, N_HEADS // n_kv, HEAD_DIM)
    s = jnp.einsum('qgjd,kgd->gjqk', qg, k).astype(jnp.float32)
    if bias is not None:
        s = s + bias
    if sink is None:
        p = masked_softmax(s, mask)
    else:
        s = jnp.where(mask, s, -jnp.inf)
        sk = jnp.broadcast_to(sink.astype(jnp.float32).reshape(n_kv, -1, 1, 1), s.shape[:-1] + (1,))
        p = jax.nn.softmax(jnp.concatenate([s, sk], axis=-1), axis=-1)[..., :-1]
    o = jnp.einsum('gjqk,kgd->qgjd', p.astype(v.dtype), v)
    return o.reshape(n_q, N_HEADS, HEAD_DIM)


def window_mask(q_pos, k_pos, win):
    d = q_pos[:, None] - k_pos[None, :]
    return (d >= 0) & (d <= win) & (k_pos[None, :] >= 0)


def sweep_query_blocks(body, n_tok):
    out = lax.map(body, jnp.arange(n_tok // Q_BLOCK) * Q_BLOCK)
    out = jnp.moveaxis(out, 0, 1)
    return out.reshape((out.shape[0], n_tok) + out.shape[3:])


def nsa_project(h, pos, wq, wkv):
    b, t, _ = h.shape
    q = rope((h @ wq).reshape(b, t, N_HEADS, HEAD_DIM), pos) * ATTN_SCALE
    kv = rope_key_slots((h @ wkv).reshape(b, t, 6, NSA_KV, HEAD_DIM), pos)
    return q, kv


def nsa_compress(rows, w):
    n = rows.shape[0] // NSA_CMP
    return jnp.einsum('nlcgd,lcgd->ncgd', rows.reshape(n, NSA_CMP, 2, NSA_KV, HEAD_DIM), w)


def nsa_cmp_slc(q, q_pos, kvc, n_sel, gather_sel):
    n_q, n_c = q.shape[0], kvc.shape[0]
    qg = q.reshape(n_q, NSA_KV, NSA_HPG, HEAD_DIM)
    c_end = (jnp.arange(n_c) + 1) * NSA_CMP - 1
    p_c = masked_softmax(jnp.einsum('qgjd,ngd->gjqn', qg, kvc[:, 0]), c_end[None, :] <= q_pos[:, None])
    o_cmp = jnp.einsum('gjqn,ngd->qgjd', p_c.astype(kvc.dtype), kvc[:, 1])
    n_cand = max(n_sel, NSA_TOPK)
    imp = jnp.pad(p_c.sum(axis=1), ((0, 0), (0, 0), (0, n_cand - n_c)))
    blk = jnp.arange(n_cand)[None, None, :]
    cur = (q_pos // NSA_SLC)[None, :, None]
    forced = (blk == 0) | (blk == cur) | (blk == cur - 1)
    score = jnp.where(blk <= cur, imp + NSA_FORCE * forced, -jnp.inf)
    idx = lax.top_k(score, NSA_TOPK)[1]
    kv_sel = gather_sel(jnp.minimum(idx, n_sel - 1))
    k_pos = idx[..., None] * NSA_SLC + jnp.arange(NSA_SLC)
    m_sel = NSA_TOPK * NSA_SLC
    mask_s = (k_pos <= q_pos[None, :, None, None]).reshape(NSA_KV, 1, n_q, m_sel)
    k_sel = kv_sel[..., 0, :].reshape(NSA_KV, n_q, m_sel, HEAD_DIM)
    v_sel = kv_sel[..., 1, :].reshape(NSA_KV, n_q, m_sel, HEAD_DIM)
    p_s = masked_softmax(jnp.einsum('qgjd,gqmd->gjqm', qg, k_sel), mask_s)
    o_slc = jnp.einsum('gjqm,gqmd->qgjd', p_s.astype(v_sel.dtype), v_sel)
    return o_cmp.reshape(n_q, N_HEADS, HEAD_DIM), o_slc.reshape(n_q, N_HEADS, HEAD_DIM)


def nsa_merge(h, o, wg, bg, wo):
    b, t, _ = h.shape
    gates = jax.nn.sigmoid((h @ wg + bg).astype(jnp.float32)).reshape(b, t, 3, N_HEADS)
    o = jnp.einsum('btch,btchd->bthd', gates.astype(o.dtype), o)
    return o.reshape(b, t, N_HEADS * HEAD_DIM) @ wo


def nsa_prompt(h, pos, wq, wkv, w_cmp, wg, bg, wo):
    b, t, _ = h.shape
    q, kv = nsa_project(h, pos, wq, wkv)
    kvc = jax.vmap(nsa_compress, (0, None))(kv[:, :, 0:2], w_cmp)
    slc_blocks = kv[:, :, 2:4].reshape(b, t // NSA_SLC, NSA_SLC, 2, NSA_KV, HEAD_DIM)
    win_pad = jnp.pad(kv[:, :, 4:6], ((0, 0), (NSA_WIN, 0), (0, 0), (0, 0), (0, 0)))
    n_sel = t // NSA_SLC
    gidx = jnp.arange(NSA_KV)[:, None, None]

    def per_seq(qb, qp, kvc_b, blocks_b, wk_b, kp):
        o_cmp, o_slc = nsa_cmp_slc(qb, qp, kvc_b, n_sel, lambda idx: blocks_b[idx, :, :, gidx])
        o_win = attend(qb, wk_b[:, 0], wk_b[:, 1], window_mask(qp, kp, NSA_WIN))
        return jnp.stack([o_cmp, o_slc, o_win], axis=1)

    def body(q0):
        qb = lax.dynamic_slice_in_dim(q, q0, Q_BLOCK, 1)
        qp = q0 + jnp.arange(Q_BLOCK)
        wk = lax.dynamic_slice_in_dim(win_pad, q0, Q_BLOCK + NSA_WIN, 1)
        kp = q0 - NSA_WIN + jnp.arange(Q_BLOCK + NSA_WIN)
        return jax.vmap(per_seq, (0, None, 0, 0, 0, None))(qb, qp, kvc, slc_blocks, wk, kp)

    o = sweep_query_blocks(body, t)
    y = nsa_merge(h, o, wg, bg, wo)
    return y, kv[:, :, 0:4], kv[:, t - min(NSA_WIN, t):, 4:6]


def nsa_sample(h, win_buf, page_table, cache, layer, wq, wkv, w_cmp, wg, bg, wo):
    n_s = h.shape[1]
    pos = PAST_LEN + jnp.arange(n_s)
    q, kv = nsa_project(h, pos, wq, wkv)
    wb_len = win_buf.shape[1]
    n_past_blk = PAST_LEN // NSA_SLC
    n_new_blk = -(-n_s // NSA_SLC)
    n_new_cmp = n_s // NSA_CMP
    n_sel = n_past_blk + n_new_blk
    blk_per_page = PAGE_SIZE // NSA_SLC
    gidx3 = jnp.arange(NSA_KV)[:, None, None]
    gidx4 = jnp.arange(NSA_KV)[:, None, None, None]
    kp_win = PAST_LEN - wb_len + jnp.arange(wb_len + n_s)

    def per_seq(args):
        qb, kvb, wb, prow = args
        past_c = cache[layer, prow, :, 0:2].reshape(PAST_LEN, 2, NSA_KV, HEAD_DIM)
        kvc = nsa_compress(past_c, w_cmp)
        if n_new_cmp > 0:
            kvc = jnp.concatenate([kvc, nsa_compress(kvb[:n_new_cmp * NSA_CMP, 0:2], w_cmp)], axis=0)
        new_blocks = jnp.pad(kvb[:, 2:4], ((0, n_new_blk * NSA_SLC - n_s), (0, 0), (0, 0), (0, 0)))
        new_blocks = new_blocks.reshape(n_new_blk, NSA_SLC, 2, NSA_KV, HEAD_DIM)

        def gather_sel(idx):
            in_past = idx < n_past_blk
            ip = jnp.minimum(idx, n_past_blk - 1)
            phys = prow[ip // blk_per_page][..., None]
            rows = (ip % blk_per_page)[..., None] * NSA_SLC + jnp.arange(NSA_SLC)
            past = cache[layer, phys, rows, 2:4, gidx4]
            new = new_blocks[jnp.clip(idx - n_past_blk, 0, n_new_blk - 1), :, :, gidx3]
            return jnp.where(in_past[..., None, None, None], past, new)

        o_cmp, o_slc = nsa_cmp_slc(qb, pos, kvc, n_sel, gather_sel)
        kw = jnp.concatenate([wb, kvb[:, 4:6]], axis=0)
        o_win = attend(qb, kw[:, 0], kw[:, 1], window_mask(pos, kp_win, NSA_WIN))
        return jnp.stack([o_cmp, o_slc, o_win], axis=1)

    o = lax.map(per_seq, (q, kv, win_buf, page_table))
    y = nsa_merge(h, o, wg, bg, wo)
    new_win = jnp.concatenate([win_buf, kv[:, :, 4:6]], axis=1)[:, -wb_len:]
    return y, kv[:, :, 0:4], new_win


def swa_project(h, pos, wq, wkv):
    b, t, _ = h.shape
    q = rope((h @ wq).reshape(b, t, N_HEADS, HEAD_DIM), pos) * ATTN_SCALE
    kv = rope_key_slots((h @ wkv).reshape(b, t, 2, SWA_KV, HEAD_DIM), pos)
    return q, kv


def swa_prompt(h, pos, wq, wkv, sink, wo):
    b, t, _ = h.shape
    q, kv = swa_project(h, pos, wq, wkv)
    kv_pad = jnp.pad(kv, ((0, 0), (SWA_WIN, 0), (0, 0), (0, 0), (0, 0)))

    def body(q0):
        qb = lax.dynamic_slice_in_dim(q, q0, Q_BLOCK, 1)
        kb = lax.dynamic_slice_in_dim(kv_pad, q0, Q_BLOCK + SWA_WIN, 1)
        mask = window_mask(q0 + jnp.arange(Q_BLOCK), q0 - SWA_WIN + jnp.arange(Q_BLOCK + SWA_WIN), SWA_WIN)
        return jax.vmap(lambda qq, kk: attend(qq, kk[:, 0], kk[:, 1], mask, sink=sink))(qb, kb)

    o = sweep_query_blocks(body, t)
    y = o.reshape(b, t, N_HEADS * HEAD_DIM) @ wo
    return y, kv[:, t - min(SWA_WIN, t):]


def swa_sample(h, buf, wq, wkv, sink, wo):
    db, n_s, _ = h.shape
    pos = PAST_LEN + jnp.arange(n_s)
    q, kv = swa_project(h, pos, wq, wkv)
    wb_len = buf.shape[1]
    k_all = jnp.concatenate([buf, kv], axis=1)
    mask = window_mask(pos, PAST_LEN - wb_len + jnp.arange(wb_len + n_s), SWA_WIN)
    o = jax.vmap(lambda qq, kk: attend(qq, kk[:, 0], kk[:, 1], mask, sink=sink))(q, k_all)
    y = o.reshape(db, n_s, N_HEADS * HEAD_DIM) @ wo
    return y, k_all[:, -wb_len:]


def fox_project(h, wq, wkv, wf, bf):
    b, t, _ = h.shape
    q = (h @ wq).reshape(b, t, N_HEADS, HEAD_DIM) * ATTN_SCALE
    kv = (h @ wkv).reshape(b, t, 2, FOX_KV, HEAD_DIM)
    logf = jax.nn.log_sigmoid((h @ wf + bf).astype(jnp.float32))
    return q, kv, logf


def fox_attend(q, d_q, kv, d_k, mask):
    hpg = N_HEADS // FOX_KV
    bias = d_q.T.reshape(FOX_KV, hpg, -1, 1) - d_k.T.reshape(FOX_KV, hpg, 1, -1)
    return attend(q, kv[:, 0], kv[:, 1], mask, bias=bias)


def fox_prompt(h, wq, wkv, wf, bf, wo):
    b, t, _ = h.shape
    q, kv, logf = fox_project(h, wq, wkv, wf, bf)
    d_cum = jnp.cumsum(logf, axis=1)
    k_pos = jnp.arange(t)

    def body(q0):
        qb = lax.dynamic_slice_in_dim(q, q0, Q_BLOCK, 1)
        db = lax.dynamic_slice_in_dim(d_cum, q0, Q_BLOCK, 1)
        mask = k_pos[None, :] <= (q0 + jnp.arange(Q_BLOCK))[:, None]
        return jax.vmap(lambda qq, dq, kk, dk: fox_attend(qq, dq, kk, dk, mask))(qb, db, kv, d_cum)

    o = sweep_query_blocks(body, t)
    y = o.reshape(b, t, N_HEADS * HEAD_DIM) @ wo
    return y, kv, logf.astype(h.dtype)


def fox_sample(h, page_table, cache_kv, cache_logf, layer, wq, wkv, wf, bf, wo):
    db, n_s, _ = h.shape
    q, kv, logf = fox_project(h, wq, wkv, wf, bf)
    mask = jnp.arange(PAST_LEN + n_s)[None, :] <= (PAST_LEN + jnp.arange(n_s))[:, None]

    def per_seq(args):
        qq, kvn, lfn, prow = args
        kv_past = cache_kv[layer, prow].reshape(PAST_LEN, 2, FOX_KV, HEAD_DIM)
        lf_past = cache_logf[layer, prow].reshape(PAST_LEN, N_HEADS).astype(jnp.float32)
        d_k = jnp.cumsum(jnp.concatenate([lf_past, lfn], axis=0), axis=0)
        k_all = jnp.concatenate([kv_past, kvn], axis=0)
        return fox_attend(qq, d_k[PAST_LEN:], k_all, d_k, mask)

    o = lax.map(per_seq, (q, kv, logf, page_table))
    y = o.reshape(db, n_s, N_HEADS * HEAD_DIM) @ wo
    return y, kv, logf.astype(h.dtype)


def conv_ffn(h, conv_state, w_in, conv_w, conv_b, w_down):
    t = h.shape[1]
    gu = h @ w_in
    g, u = gu[..., :D_FF], gu[..., D_FF:]
    gp = jnp.concatenate([conv_state.astype(g.dtype), g], axis=1)
    gc = conv_b + conv_w[0] * gp[:, 0:t]
    for j in range(1, CONV_W):
        gc = gc + conv_w[j] * gp[:, j:j + t]
    a = jax.nn.silu(gc) * u
    return a @ w_down, gp[:, -(CONV_W - 1):]


def setup_inputs(seed: int = 0) -> dict:
    key = jax.random.key(seed)
    keys = jax.random.split(key, 32)
    n_pages = PAST_LEN // PAGE_SIZE
    n_pool = (DEC_BATCH * n_pages * 5) // 4
    D, H, HD = D_MODEL, N_HEADS, HEAD_DIM

    def nrm(i, shape, scale=1.0):
        return scale * jax.random.normal(keys[i], shape, jnp.float32)

    page_table = jax.random.permutation(keys[2], n_pool)[:DEC_BATCH * n_pages]
    page_table = page_table.reshape(DEC_BATCH, n_pages).astype(jnp.int32)
    return {
        "x_prompt": nrm(0, (BATCH, SEQ, D)),
        "x_sample": nrm(1, (DEC_BATCH, DEC_SEQ, D)),
        "page_table": page_table,
        "cache_nsa_kv": nrm(3, (N_LAYERS_A, n_pool, PAGE_SIZE, 4, NSA_KV, HD)),
        "state_nsa_win": nrm(4, (N_LAYERS_A, DEC_BATCH, min(NSA_WIN, PAST_LEN), 2, NSA_KV, HD)),
        "state_swa_kv": nrm(5, (N_LAYERS_B, DEC_BATCH, min(SWA_WIN, PAST_LEN), 2, SWA_KV, HD)),
        "cache_fox_kv": nrm(6, (N_LAYERS_C, n_pool, PAGE_SIZE, 2, FOX_KV, HD)),
        "cache_fox_logf": jax.nn.log_sigmoid(FOX_GATE_BIAS + nrm(7, (N_LAYERS_C, n_pool, PAGE_SIZE, H), 0.5)),
        "state_ffn_conv": nrm(8, (DEPTH, DEC_BATCH, CONV_W - 1, D_FF)),
        "norm_mix": 1.0 + nrm(9, (DEPTH, D), 0.01),
        "norm_ffn": 1.0 + nrm(10, (DEPTH, D), 0.01),
        "norm_final": 1.0 + nrm(11, (D,), 0.01),
        "nsa_wq": nrm(12, (N_LAYERS_A, D, H * HD), D ** -0.5),
        "nsa_wkv": nrm(13, (N_LAYERS_A, D, 6 * NSA_KV * HD), D ** -0.5),
        "nsa_cmp_w": nrm(14, (N_LAYERS_A, NSA_CMP, 2, NSA_KV, HD), NSA_CMP ** -0.5),
        "nsa_wg": nrm(15, (N_LAYERS_A, D, 3 * H), D ** -0.5),
        "nsa_bg": nrm(16, (N_LAYERS_A, 3 * H), 0.01),
        "nsa_wo": nrm(17, (N_LAYERS_A, H * HD, D), (H * HD) ** -0.5),
        "swa_wq": nrm(18, (N_LAYERS_B, D, H * HD), D ** -0.5),
        "swa_wkv": nrm(19, (N_LAYERS_B, D, 2 * SWA_KV * HD), D ** -0.5),
        "swa_sink": nrm(20, (N_LAYERS_B, H)),
        "swa_wo": nrm(21, (N_LAYERS_B, H * HD, D), (H * HD) ** -0.5),
        "fox_wq": nrm(22, (N_LAYERS_C, D, H * HD), D ** -0.5),
        "fox_wkv": nrm(23, (N_LAYERS_C, D, 2 * FOX_KV * HD), D ** -0.5),
        "fox_wf": nrm(24, (N_LAYERS_C, D, H), 0.5 * D ** -0.5),
        "fox_bf": FOX_GATE_BIAS + nrm(25, (N_LAYERS_C, H), 0.5),
        "fox_wo": nrm(26, (N_LAYERS_C, H * HD, D), (H * HD) ** -0.5),
        "ffn_w_in": nrm(27, (DEPTH, D, 2 * D_FF), D ** -0.5),
        "ffn_conv_w": nrm(28, (DEPTH, CONV_W, D_FF), CONV_W ** -0.5),
        "ffn_conv_b": nrm(29, (DEPTH, D_FF), 0.01),
        "ffn_w_down": nrm(30, (DEPTH, D_FF, D), D_FF ** -0.5),
    }


def reference(x_prompt, x_sample, page_table, cache_nsa_kv, state_nsa_win, state_swa_kv,
              cache_fox_kv, cache_fox_logf, state_ffn_conv, norm_mix, norm_ffn, norm_final,
              nsa_wq, nsa_wkv, nsa_cmp_w, nsa_wg, nsa_bg, nsa_wo,
              swa_wq, swa_wkv, swa_sink, swa_wo,
              fox_wq, fox_wkv, fox_wf, fox_bf, fox_wo,
              ffn_w_in, ffn_conv_w, ffn_conv_b, ffn_w_down):
    pos_p = jnp.arange(x_prompt.shape[1])
    xp, xs = x_prompt, x_sample
    nsa_kv_p, nsa_kv_s, nsa_win_p, nsa_win_s = [], [], [], []
    swa_kv_p, swa_kv_s = [], []
    fox_kv_p, fox_kv_s, fox_lf_p, fox_lf_s = [], [], [], []
    conv_p, conv_s = [], []
    for i in range(DEPTH):
        j = i // N_MIXERS
        hp = rms_norm(xp, norm_mix[i])
        hs = rms_norm(xs, norm_mix[i])
        kind = i % N_MIXERS
        if kind == 0:
            yp, rp, wp = nsa_prompt(hp, pos_p, nsa_wq[j], nsa_wkv[j], nsa_cmp_w[j], nsa_wg[j], nsa_bg[j], nsa_wo[j])
            ys, rs, ws = nsa_sample(hs, state_nsa_win[j], page_table, cache_nsa_kv, j,
                                    nsa_wq[j], nsa_wkv[j], nsa_cmp_w[j], nsa_wg[j], nsa_bg[j], nsa_wo[j])
            nsa_kv_p.append(rp); nsa_kv_s.append(rs); nsa_win_p.append(wp); nsa_win_s.append(ws)
        elif kind == 1:
            yp, bp = swa_prompt(hp, pos_p, swa_wq[j], swa_wkv[j], swa_sink[j], swa_wo[j])
            ys, bs = swa_sample(hs, state_swa_kv[j], swa_wq[j], swa_wkv[j], swa_sink[j], swa_wo[j])
            swa_kv_p.append(bp); swa_kv_s.append(bs)
        else:
            yp, kp, lp = fox_prompt(hp, fox_wq[j], fox_wkv[j], fox_wf[j], fox_bf[j], fox_wo[j])
            ys, ks, ls = fox_sample(hs, page_table, cache_fox_kv, cache_fox_logf, j,
                                    fox_wq[j], fox_wkv[j], fox_wf[j], fox_bf[j], fox_wo[j])
            fox_kv_p.append(kp); fox_kv_s.append(ks); fox_lf_p.append(lp); fox_lf_s.append(ls)
        xp = xp + yp
        xs = xs + ys
        zero_state = jnp.zeros((xp.shape[0], CONV_W - 1, D_FF), xp.dtype)
        fp, cp = conv_ffn(rms_norm(xp, norm_ffn[i]), zero_state, ffn_w_in[i], ffn_conv_w[i], ffn_conv_b[i], ffn_w_down[i])
        fs, cs = conv_ffn(rms_norm(xs, norm_ffn[i]), state_ffn_conv[i], ffn_w_in[i], ffn_conv_w[i], ffn_conv_b[i], ffn_w_down[i])
        xp = xp + fp
        xs = xs + fs
        conv_p.append(cp); conv_s.append(cs)
    y_prompt = rms_norm(xp, norm_final)
    y_sample = rms_norm(xs, norm_final)
    return (y_prompt, y_sample,
            jnp.stack(nsa_kv_p), jnp.stack(nsa_kv_s), jnp.stack(nsa_win_p), jnp.stack(nsa_win_s),
            jnp.stack(swa_kv_p), jnp.stack(swa_kv_s),
            jnp.stack(fox_kv_p), jnp.stack(fox_kv_s), jnp.stack(fox_lf_p), jnp.stack(fox_lf_s),
            jnp.stack(conv_p), jnp.stack(conv_s))
```

```python
import functools
import math

import jax
import jax.numpy as jnp
from jax import lax
from jax.experimental import pallas as pl
from jax.experimental.pallas import tpu as pltpu

F32 = jnp.float32
BF16 = jnp.bfloat16
HIGHEST = lax.Precision.HIGHEST

HEAD_DIM = 64
N_HEADS = 16
ATTN_SCALE = HEAD_DIM ** -0.5
ROPE_THETA = 10000.0
RMS_EPS = 1e-6
N_MIXERS = 3
NSA_KV = 2
NSA_HPG = N_HEADS // NSA_KV
NSA_BLK = 64
NSA_TOPK = 16
NSA_N_FORCED = 3
NSA_WIN = 512
SWA_KV = 2
SWA_WIN = 128
FOX_KV = 4
CONV_W = 3
LANES = 128
SUBLANES = 8
NEG = -1e30
VMEM_LIMIT = 56 * 1024 * 1024


def _cparams(sem):
    return pltpu.CompilerParams(dimension_semantics=sem, vmem_limit_bytes=VMEM_LIMIT)


def _resident(shape, index_map):
    return pl.BlockSpec(shape, index_map, pipeline_mode=pl.Buffered(1))


def _dot(a, b):
    return jnp.dot(a, b, preferred_element_type=F32)


def _dot_nt(a, b, precision=None):
    return lax.dot_general(a, b, (((1,), (1,)), ((), ())), precision=precision,
                           preferred_element_type=F32)


def _rms(x, g):
    ms = jnp.mean(x * x, axis=-1, keepdims=True)
    return x * lax.rsqrt(ms + RMS_EPS) * g


def _rope_chunk(y, cos, sin_signed, first_half):
    swapped = jnp.where(first_half, pltpu.roll(y, LANES - 32, 1), pltpu.roll(y, 32, 1))
    return y * cos + swapped * sin_signed


def _log_sigmoid(z):
    return jnp.minimum(z, 0.0) - jnp.log(1.0 + jnp.exp(-jnp.abs(z)))


def _osm_init(rows):
    return (jnp.full((rows, 1), NEG, F32), jnp.zeros((rows, 1), F32), jnp.zeros((rows, HEAD_DIM), F32))


def _osm_step(carry, s, mask, v):
    m, l, acc = carry
    s = jnp.where(mask, s, NEG)
    m_new = jnp.maximum(m, jnp.max(s, axis=-1, keepdims=True))
    p = jnp.where(mask, jnp.exp(s - m_new), 0.0)
    alpha = jnp.exp(m - m_new)
    l = alpha * l + jnp.sum(p, axis=-1, keepdims=True)
    acc = alpha * acc + _dot(p.astype(BF16), v)
    return m_new, l, acc


def _osm_finish(carry):
    m, l, acc = carry
    return acc / jnp.maximum(l, 1e-30)


def _stack_heads(q, g, hpg):
    return jnp.concatenate(
        [q[:, (g * hpg + j) * HEAD_DIM:(g * hpg + j + 1) * HEAD_DIM] for j in range(hpg)], axis=0)


def _tile_rows(x, reps):
    return jnp.concatenate([x] * reps, axis=0)


def _proj_kernel(*refs, mode, has_rope):
    it = iter(refs)
    x_ref, g_ref, w_ref = next(it), next(it), next(it)
    if has_rope:
        cos_ref, sin_ref = next(it), next(it)
    if mode == "nsa":
        bg_ref = next(it)
        q_out, kv_out, win_out, kvbf_out, gate_out = (next(it) for _ in range(5))
    elif mode == "swa":
        q_out, kv_out, kvbf_out = (next(it) for _ in range(3))
    else:
        wf_ref, wft_ref, bf_ref, bft_ref = (next(it) for _ in range(4))
        q_out, kv_out, kvbf_out, lf_out, lft_out = (next(it) for _ in range(5))

    hn = _rms(x_ref[...], g_ref[...])
    y = _dot(hn.astype(BF16), w_ref[...])
    if has_rope:
        cos, sin = cos_ref[...], sin_ref[...]
        first_half = (lax.broadcasted_iota(jnp.int32, cos.shape, 1) % HEAD_DIM) < HEAD_DIM // 2

    def chunk(c, rope):
        yc = y[:, c * LANES:(c + 1) * LANES]
        return _rope_chunk(yc, cos, sin, first_half) if rope else yc

    n_q = N_HEADS * HEAD_DIM // LANES
    for c in range(n_q):
        q_out[:, c * LANES:(c + 1) * LANES] = (chunk(c, has_rope) * ATTN_SCALE).astype(BF16)
    if mode == "nsa":
        for c in range(6):
            yc = chunk(n_q + c, c % 2 == 0)
            if c < 4:
                kv_out[:, c * LANES:(c + 1) * LANES] = yc
            else:
                win_out[:, (c - 4) * LANES:(c - 3) * LANES] = yc
            if c >= 2:
                kvbf_out[:, (c - 2) * LANES:(c - 1) * LANES] = yc.astype(BF16)
        gate_out[...] = jax.nn.sigmoid(chunk(n_q + 6, False) + bg_ref[...])
    elif mode == "swa":
        for c in range(2):
            yc = chunk(n_q + c, c == 0)
            kv_out[:, c * LANES:(c + 1) * LANES] = yc
            kvbf_out[:, c * LANES:(c + 1) * LANES] = yc.astype(BF16)
    else:
        for c in range(4):
            yc = chunk(n_q + c, False)
            kv_out[:, c * LANES:(c + 1) * LANES] = yc
            kvbf_out[:, c * LANES:(c + 1) * LANES] = yc.astype(BF16)
        z = jnp.dot(hn, wf_ref[...], precision=HIGHEST, preferred_element_type=F32) + bf_ref[...]
        lf_out[...] = _log_sigmoid(z)
        zt = _dot_nt(wft_ref[...], hn, precision=HIGHEST) + bft_ref[...]
        lft_out[0] = _log_sigmoid(zt)


def _rope_tables(pos):
    half = HEAD_DIM // 2
    inv_freq = jnp.exp(-math.log(ROPE_THETA) * jnp.arange(half, dtype=F32) / half)
    ang = pos.astype(F32)[:, None] * inv_freq[None, :]
    cos, sin = jnp.cos(ang), jnp.sin(ang)
    return jnp.tile(cos, (1, 4)), jnp.concatenate([-sin, sin, -sin, sin], axis=1)


def _project(x, seq_len, pos, g, w_cat, mode, extras):
    n, d = x.shape
    tm = min(512, n)
    assert n % tm == 0 and (seq_len % tm == 0 or tm % seq_len == 0)
    has_rope = mode != "fox"
    c_tot = w_cat.shape[1]
    in_specs = [pl.BlockSpec((tm, d), lambda i: (i, 0)),
                _resident((1, d), lambda i: (0, 0)),
                _resident((d, c_tot), lambda i: (0, 0))]
    args = [x, g.reshape(1, d), w_cat]
    if has_rope:
        cos, sin = _rope_tables(pos)
        if seq_len < tm:
            cos, sin = jnp.tile(cos, (tm // seq_len, 1)), jnp.tile(sin, (tm // seq_len, 1))
        period = cos.shape[0] // tm
        in_specs += [pl.BlockSpec((tm, LANES), lambda i: (i % period, 0))] * 2
        args += [cos, sin]
    qd = N_HEADS * HEAD_DIM
    row = lambda w: pl.BlockSpec((tm, w), lambda i: (i, 0))
    if mode == "nsa":
        in_specs.append(_resident((1, LANES), lambda i: (0, 0)))
        args.append(extras["bg"])
        widths = [(qd, BF16), (512, F32), (256, F32), (512, BF16), (LANES, F32)]
        out_specs = [row(w) for w, _ in widths]
        out_shape = [jax.ShapeDtypeStruct((n, w), dt) for w, dt in widths]
    elif mode == "swa":
        widths = [(qd, BF16), (256, F32), (256, BF16)]
        out_specs = [row(w) for w, _ in widths]
        out_shape = [jax.ShapeDtypeStruct((n, w), dt) for w, dt in widths]
    else:
        wf, bf = extras["wf"], extras["bf"]
        in_specs += [_resident((d, N_HEADS), lambda i: (0, 0)),
                     _resident((N_HEADS, d), lambda i: (0, 0)),
                     _resident((1, N_HEADS), lambda i: (0, 0)),
                     _resident((N_HEADS, 1), lambda i: (0, 0))]
        args += [wf, wf.T, bf.reshape(1, N_HEADS), bf.reshape(N_HEADS, 1)]
        widths = [(qd, BF16), (512, F32), (512, BF16), (N_HEADS, F32)]
        out_specs = [row(w) for w, _ in widths]
        out_shape = [jax.ShapeDtypeStruct((n, w), dt) for w, dt in widths]
        out_specs.append(pl.BlockSpec((1, N_HEADS, tm), lambda i: (i, 0, 0)))
        out_shape.append(jax.ShapeDtypeStruct((n // tm, N_HEADS, tm), F32))
    return pl.pallas_call(
        functools.partial(_proj_kernel, mode=mode, has_rope=has_rope),
        grid=(n // tm,), in_specs=in_specs, out_specs=out_specs, out_shape=out_shape,
        compiler_params=_cparams(("parallel",)), name=f"proj_{mode}")(*args)


def _post_kernel(*refs, per_seq_tiles, grouped, n_chunks):
    it = iter(refs)
    x_ref, o_ref, wo_ref, g_ref, win_ref, cw_ref, cb_ref, wd_ref = (next(it) for _ in range(8))
    if grouped:
        s0_ref, s1_ref = next(it), next(it)
    xo_ref, gtail_ref = next(it), next(it)
    if not grouped:
        carry_ref = next(it)

    tm = x_ref.shape[0]
    d_ff = wd_ref.shape[0]
    fc = d_ff // n_chunks
    x1 = x_ref[...] + _dot(o_ref[...], wo_ref[...])
    hb = _rms(x1, g_ref[...]).astype(BF16)
    row = lax.broadcasted_iota(jnp.int32, (tm, 1), 0)
    if grouped:
        r_in = row % SUBLANES
    else:
        r_in = row
        first_tile = pl.program_id(0) % per_seq_tiles == 0
    acc = jnp.zeros_like(x1)
    for c in range(n_chunks):
        cols = slice(c * fc, (c + 1) * fc)
        gch = _dot(hb, win_ref[:, c * fc:(c + 1) * fc])
        uch = _dot(hb, win_ref[:, d_ff + c * fc:d_ff + (c + 1) * fc])
        if grouped:
            prev1 = s1_ref[:, cols]
            prev2a, prev2b = s0_ref[:, cols], s1_ref[:, cols]
        else:
            prev2a = jnp.where(first_tile, 0.0, carry_ref[0:1, cols])
            prev1 = prev2b = jnp.where(first_tile, 0.0, carry_ref[1:2, cols])
        gm1 = jnp.where(r_in == 0, prev1, pltpu.roll(gch, 1, 0))
        gm2 = jnp.where(r_in == 0, prev2a, jnp.where(r_in == 1, prev2b, pltpu.roll(gch, 2, 0)))
        gc = cb_ref[:, cols] + cw_ref[0:1, cols] * gm2 + cw_ref[1:2, cols] * gm1 + cw_ref[2:3, cols] * gch
        a = gc * jax.nn.sigmoid(gc) * uch
        acc = acc + _dot(a.astype(BF16), wd_ref[c * fc:(c + 1) * fc, :])
        if grouped:
            gtail_ref[:, cols] = gch
        else:
            gtail_ref[0, :, cols] = gch[tm - SUBLANES:, :]
            carry_ref[0:2, cols] = gch[tm - 2:, :]
    xo_ref[...] = x1 + acc


def _post(x, o, seq_len, wo, g, w_in, conv_w, conv_b, w_down, conv_state):
    n, d = x.shape
    d_ff = w_down.shape[0]
    grouped = conv_state is not None
    tm = min(256 if grouped else 512, n)
    n_seq = n // seq_len
    in_specs = [pl.BlockSpec((tm, d), lambda i: (i, 0)),
                pl.BlockSpec((tm, o.shape[1]), lambda i: (i, 0)),
                _resident(wo.shape, lambda i: (0, 0)),
                _resident((1, d), lambda i: (0, 0)),
                _resident(w_in.shape, lambda i: (0, 0)),
                _resident((CONV_W, d_ff), lambda i: (0, 0)),
                _resident((1, d_ff), lambda i: (0, 0)),
                _resident(w_down.shape, lambda i: (0, 0))]
    args = [x, o, wo, g.reshape(1, d), w_in, conv_w, conv_b.reshape(1, d_ff), w_down]
    scratch = []
    if grouped:
        assert seq_len == SUBLANES
        in_specs += [pl.BlockSpec((tm, d_ff), lambda i: (i, 0))] * 2
        args += [jnp.repeat(conv_state[:, 0], seq_len, axis=0), jnp.repeat(conv_state[:, 1], seq_len, axis=0)]
        tail_spec = pl.BlockSpec((tm, d_ff), lambda i: (i, 0))
        tail_shape = jax.ShapeDtypeStruct((n, d_ff), F32)
        per_seq_tiles = 1
    else:
        assert seq_len % tm == 0
        per_seq_tiles = seq_len // tm
        tail_spec = pl.BlockSpec((1, SUBLANES, d_ff), lambda i: (i // per_seq_tiles, 0, 0))
        tail_shape = jax.ShapeDtypeStruct((n_seq, SUBLANES, d_ff), F32)
        scratch = [pltpu.VMEM((SUBLANES, d_ff), F32)]
    x_new, tail = pl.pallas_call(
        functools.partial(_post_kernel, per_seq_tiles=per_seq_tiles, grouped=grouped, n_chunks=2),
        grid=(n // tm,), in_specs=in_specs,
        out_specs=[pl.BlockSpec((tm, d), lambda i: (i, 0)), tail_spec],
        out_shape=[jax.ShapeDtypeStruct((n, d), F32), tail_shape],
        scratch_shapes=scratch,
        compiler_params=_cparams(("arbitrary",)), name="post_ffn")(*args)
    if grouped:
        conv_new = tail.reshape(n_seq, seq_len, d_ff)[:, seq_len - (CONV_W - 1):]
    else:
        conv_new = tail[:, SUBLANES - (CONV_W - 1):]
    return x_new, conv_new


def _final_norm_kernel(x_ref, g_ref, o_ref):
    o_ref[...] = _rms(x_ref[...], g_ref[...])


def _final_norm(x, g):
    n, d = x.shape
    tm = min(1024, n)
    return pl.pallas_call(
        _final_norm_kernel, grid=(n // tm,),
        in_specs=[pl.BlockSpec((tm, d), lambda i: (i, 0)), _resident((1, d), lambda i: (0, 0))],
        out_specs=pl.BlockSpec((tm, d), lambda i: (i, 0)),
        out_shape=jax.ShapeDtypeStruct((n, d), F32),
        compiler_params=_cparams(("parallel",)), name="final_norm")(x, g.reshape(1, d))


def _stream_attention(qg, kv_tile, lo, hi, mask_fn, bias_fn=None):
    def body(t, carry):
        k, v = kv_tile(t)
        s = _dot_nt(qg, k)
        if bias_fn is not None:
            s = s + bias_fn(t)
        return _osm_step(carry, s, mask_fn(t), v)

    return lax.fori_loop(lo, hi, body, _osm_init(qg.shape[0]))


def _positions(q0, qb, hpg, tk):
    r = lax.broadcasted_iota(jnp.int32, (hpg * qb, 1), 0)
    qpos = q0 + r % qb
    lane = lax.broadcasted_iota(jnp.int32, (1, tk), 1)
    return qpos, (lambda t: t * tk + lane)


def _head_columns(x, first, hpg, stride=1):
    return jnp.concatenate([x[:, first + j * stride:first + j * stride + 1] for j in range(hpg)], axis=0)


def _store_heads(o_ref, out, g, hpg, qb):
    for j in range(hpg):
        h = g * hpg + j
        o_ref[0, :, h * HEAD_DIM:(h + 1) * HEAD_DIM] = out[j * qb:(j + 1) * qb].astype(o_ref.dtype)


def _band_kernel(sink_ref, q_ref, kv_ref, o_ref, *, window, n_kv, tk, has_sink):
    qb = q_ref.shape[1]
    hpg = N_HEADS // n_kv
    q0 = pl.program_id(1) * qb
    q = q_ref[0]
    qpos, kpos_of = _positions(q0, qb, hpg, tk)
    lo = jnp.maximum(q0 - window, 0) // tk
    hi = (q0 + qb - 1) // tk + 1
    v_off = n_kv * HEAD_DIM
    for g in range(n_kv):
        qg = _stack_heads(q, g, hpg)

        def kv_tile(t, g=g):
            rows = pl.ds(pl.multiple_of(t * tk, tk), tk)
            return (kv_ref[0, rows, g * HEAD_DIM:(g + 1) * HEAD_DIM],
                    kv_ref[0, rows, v_off + g * HEAD_DIM:v_off + (g + 1) * HEAD_DIM])

        def mask_fn(t):
            d = qpos - kpos_of(t)
            return (d >= 0) & (d <= window)

        m, l, acc = _stream_attention(qg, kv_tile, lo, hi, mask_fn)
        if has_sink:
            hrow = lax.broadcasted_iota(jnp.int32, (hpg * qb, 1), 0) // qb
            sink = jnp.zeros((hpg * qb, 1), F32)
            for j in range(hpg):
                sink = jnp.where(hrow == j, sink_ref[g * hpg + j], sink)
            m_f = jnp.maximum(m, sink)
            scale = jnp.exp(m - m_f)
            out = acc * scale / (l * scale + jnp.exp(sink - m_f))
        else:
            out = _osm_finish((m, l, acc))
        _store_heads(o_ref, out, g, hpg, qb)


def _band_attention(q, kv_bf, sink, batch, seq_len, window, n_kv, col_block, n_col_blocks):
    qd = q.shape[1]
    qb = min(128, seq_len)
    tk = min(128, seq_len)
    cw = 2 * n_kv * HEAD_DIM
    has_sink = sink is not None
    if sink is None:
        sink = jnp.zeros((N_HEADS,), F32)
    q3 = q.reshape(batch, seq_len, qd)
    kv3 = kv_bf.reshape(batch, seq_len, n_col_blocks * cw)
    out = pl.pallas_call(
        functools.partial(_band_kernel, window=window, n_kv=n_kv, tk=tk, has_sink=has_sink),
        grid=(batch, seq_len // qb),
        in_specs=[pl.BlockSpec(memory_space=pltpu.SMEM),
                  pl.BlockSpec((1, qb, qd), lambda b, i: (b, i, 0)),
                  pl.BlockSpec((1, seq_len, cw), lambda b, i: (b, 0, col_block))],
        out_specs=pl.BlockSpec((1, qb, qd), lambda b, i: (b, i, 0)),
        out_shape=jax.ShapeDtypeStruct((batch, seq_len, qd), BF16),
        compiler_params=_cparams(("parallel", "parallel")), name="band_attention")(sink, q3, kv3)
    return out.reshape(batch * seq_len, qd)


def _tri(n, upper):
    r = lax.broadcasted_iota(jnp.int32, (n, n), 0)
    c = lax.broadcasted_iota(jnp.int32, (n, n), 1)
    return jnp.where((r <= c) if upper else (r >= c), 1.0, 0.0).astype(F32)


def _fox_kernel(q_ref, kv_ref, lf_ref, lft_ref, o_ref, dt_ref, ctok_ref, ct_ref):
    qb = q_ref.shape[1]
    hpg = N_HEADS // FOX_KV
    i = pl.program_id(1)
    q0 = i * qb

    @pl.when(i == 0)
    def _():
        ctok_ref[...] = jnp.zeros_like(ctok_ref)
        ct_ref[...] = jnp.zeros_like(ct_ref)

    dq = ctok_ref[...] + jnp.dot(_tri(qb, False), lf_ref[0], precision=HIGHEST, preferred_element_type=F32)
    dt = ct_ref[...] + jnp.dot(lft_ref[0], _tri(qb, True), precision=HIGHEST, preferred_element_type=F32)
    dt_ref[i] = dt
    ctok_ref[...] = dq[qb - 1:qb, :]
    ct_ref[...] = dt[:, qb - 1:qb]

    q = q_ref[0]
    qpos, kpos_of = _positions(q0, qb, hpg, qb)
    v_off = FOX_KV * HEAD_DIM
    for g in range(FOX_KV):
        qg = _stack_heads(q, g, hpg)
        dq_col = _head_columns(dq, g * hpg, hpg)

        def kv_tile(t, g=g):
            rows = pl.ds(pl.multiple_of(t * qb, qb), qb)
            return (kv_ref[0, rows, g * HEAD_DIM:(g + 1) * HEAD_DIM],
                    kv_ref[0, rows, v_off + g * HEAD_DIM:v_off + (g + 1) * HEAD_DIM])

        def bias_fn(t, g=g, dq_col=dq_col):
            dk = dt_ref[t][g * hpg:(g + 1) * hpg, :]
            dk = jnp.broadcast_to(dk[:, None, :], (hpg, qb, qb)).reshape(hpg * qb, qb)
            return dq_col - dk

        def mask_fn(t):
            return kpos_of(t) <= qpos

        out = _osm_finish(_stream_attention(qg, kv_tile, 0, i + 1, mask_fn, bias_fn))
        _store_heads(o_ref, out, g, hpg, qb)


def _fox_attention(q, kv_bf, lf, lft, batch, seq_len):
    qd = q.shape[1]
    qb = min(256, seq_len)
    cw = kv_bf.shape[1]
    out = pl.pallas_call(
        _fox_kernel, grid=(batch, seq_len // qb),
        in_specs=[pl.BlockSpec((1, qb, qd), lambda b, i: (b, i, 0)),
                  pl.BlockSpec((1, seq_len, cw), lambda b, i: (b, 0, 0)),
                  pl.BlockSpec((1, qb, N_HEADS), lambda b, i: (b, i, 0)),
                  pl.BlockSpec((1, N_HEADS, qb), lambda b, i: (b, 0, i))],
        out_specs=pl.BlockSpec((1, qb, qd), lambda b, i: (b, i, 0)),
        out_shape=jax.ShapeDtypeStruct((batch, seq_len, qd), BF16),
        scratch_shapes=[pltpu.VMEM((seq_len // qb, N_HEADS, qb), F32),
                        pltpu.VMEM((1, N_HEADS), F32), pltpu.VMEM((N_HEADS, 1), F32)],
        compiler_params=_cparams(("parallel", "arbitrary")), name="fox_attention")(
            q.reshape(batch, seq_len, qd), kv_bf.reshape(batch, seq_len, cw),
            lf.reshape(batch, seq_len, N_HEADS), lft)
    return out.reshape(batch * seq_len, qd)


def _compress_kernel(x_ref, w_ref, o_ref):
    rows, c = x_ref.shape
    x = x_ref[...].reshape(rows // NSA_BLK, NSA_BLK, c)
    o_ref[...] = jnp.sum(x * w_ref[...][None], axis=1)


def _compress(kv_f32, w_cmp):
    n = kv_f32.shape[0]
    tm = min(512, n)
    c = 2 * NSA_KV * HEAD_DIM
    return pl.pallas_call(
        _compress_kernel, grid=(n // tm,),
        in_specs=[pl.BlockSpec((tm, c), lambda i: (i, 0)), _resident((NSA_BLK, c), lambda i: (0, 0))],
        out_specs=pl.BlockSpec((tm // NSA_BLK, c), lambda i: (i, 0)),
        out_shape=jax.ShapeDtypeStruct((n // NSA_BLK, c), F32),
        compiler_params=_cparams(("parallel",)), name="nsa_compress")(kv_f32, w_cmp.reshape(NSA_BLK, c))


def _split_bf16(x):
    hi = x.astype(BF16)
    return hi, (x - hi.astype(F32)).astype(BF16)


def _select_blocks(imp, cur, n_free):
    nblk = imp.shape[1]
    blk = lax.broadcasted_iota(jnp.int32, imp.shape, 1)
    forced = (blk == 0) | (blk == cur) | (blk == cur - 1)
    cand = jnp.where((blk <= cur) & jnp.logical_not(forced), imp, -1.0)
    sel = jnp.where(forced, 1.0, 0.0)
    blk_f = blk.astype(F32)
    for _ in range(n_free):
        m = jnp.max(cand, axis=-1, keepdims=True)
        first = jnp.min(jnp.where(cand == m, blk_f, float(nblk)), axis=-1, keepdims=True)
        hit = (blk_f == first) & (m >= 0.0)
        sel = jnp.where(hit, 1.0, sel)
        cand = jnp.where(hit, -1.0, cand)
    return sel


def _softmax_rows(s, mask):
    s = jnp.where(mask, s, NEG)
    m = jnp.max(s, axis=-1, keepdims=True)
    e = jnp.where(mask, jnp.exp(s - m), 0.0)
    return e / jnp.maximum(jnp.sum(e, axis=-1, keepdims=True), 1e-30)


def _nsa_kernel(q_ref, gate_ref, kvc_ref, kv_ref, onehot_ref, o_ref, *, tk):
    qb = q_ref.shape[1]
    hpg = NSA_HPG
    rows = hpg * qb
    nblk = kvc_ref.shape[1]
    q0 = pl.program_id(1) * qb
    q = q_ref[0]
    gates = gate_ref[0]
    qpos, kpos_of = _positions(q0, qb, hpg, tk)
    qpos1 = qpos[:qb]
    blk = lax.broadcasted_iota(jnp.int32, (1, nblk), 1)
    kv_w = NSA_KV * HEAD_DIM
    hi_tile = (q0 + qb - 1) // tk + 1
    lo_win = jnp.maximum(q0 - NSA_WIN, 0) // tk
    for g in range(NSA_KV):
        qg = _stack_heads(q, g, hpg)
        kc_hi, kc_lo = _split_bf16(kvc_ref[0, :, g * HEAD_DIM:(g + 1) * HEAD_DIM])
        vc = kvc_ref[0, :, kv_w + g * HEAD_DIM:kv_w + (g + 1) * HEAD_DIM].astype(BF16)
        s_c = _dot_nt(qg, kc_hi) + _dot_nt(qg, kc_lo)
        p_c = _softmax_rows(s_c, (blk + 1) * NSA_BLK - 1 <= qpos)
        o_cmp = _dot(p_c.astype(BF16), vc)
        imp = jnp.sum(p_c.reshape(hpg, qb, nblk), axis=0)
        sel = _select_blocks(imp, qpos1 // NSA_BLK, NSA_TOPK - NSA_N_FORCED).astype(BF16)

        def slc_tile(t, g=g):
            r = pl.ds(pl.multiple_of(t * tk, tk), tk)
            return (kv_ref[0, r, g * HEAD_DIM:(g + 1) * HEAD_DIM],
                    kv_ref[0, r, kv_w + g * HEAD_DIM:kv_w + (g + 1) * HEAD_DIM])

        def slc_mask(t, sel=sel):
            r = pl.ds(pl.multiple_of(t * tk, tk), tk)
            picked = _tile_rows(_dot_nt(sel, onehot_ref[r, :]), hpg)
            return (picked > 0.5) & (kpos_of(t) <= qpos)

        o_slc = _osm_finish(_stream_attention(qg, slc_tile, 0, hi_tile, slc_mask))

        def win_tile(t, g=g):
            r = pl.ds(pl.multiple_of(t * tk, tk), tk)
            return (kv_ref[0, r, 2 * kv_w + g * HEAD_DIM:2 * kv_w + (g + 1) * HEAD_DIM],
                    kv_ref[0, r, 3 * kv_w + g * HEAD_DIM:3 * kv_w + (g + 1) * HEAD_DIM])

        def win_mask(t):
            d = qpos - kpos_of(t)
            return (d >= 0) & (d <= NSA_WIN)

        o_win = _osm_finish(_stream_attention(qg, win_tile, lo_win, hi_tile, win_mask))
        out = (o_cmp * _head_columns(gates, g * hpg, hpg)
               + o_slc * _head_columns(gates, N_HEADS + g * hpg, hpg)
               + o_win * _head_columns(gates, 2 * N_HEADS + g * hpg, hpg))
        _store_heads(o_ref, out, g, hpg, qb)


def _block_onehot(n_keys, nblk):
    return (jnp.arange(n_keys)[:, None] // NSA_BLK == jnp.arange(nblk)[None, :]).astype(BF16)


def _nsa_attention(q, gates, kvc, kv_bf, batch, seq_len):
    qd = q.shape[1]
    qb = min(128, seq_len)
    tk = min(256, seq_len)
    nblk = seq_len // NSA_BLK
    cw = kv_bf.shape[1]
    out = pl.pallas_call(
        functools.partial(_nsa_kernel, tk=tk), grid=(batch, seq_len // qb),
        in_specs=[pl.BlockSpec((1, qb, qd), lambda b, i: (b, i, 0)),
                  pl.BlockSpec((1, qb, LANES), lambda b, i: (b, i, 0)),
                  pl.BlockSpec((1, nblk, kvc.shape[1]), lambda b, i: (b, 0, 0)),
                  pl.BlockSpec((1, seq_len, cw), lambda b, i: (b, 0, 0)),
                  _resident((seq_len, nblk), lambda b, i: (0, 0))],
        out_specs=pl.BlockSpec((1, qb, qd), lambda b, i: (b, i, 0)),
        out_shape=jax.ShapeDtypeStruct((batch, seq_len, qd), BF16),
        compiler_params=_cparams(("parallel", "parallel")), name="nsa_attention")(
            q.reshape(batch, seq_len, qd), gates.reshape(batch, seq_len, LANES),
            kvc.reshape(batch, nblk, kvc.shape[1]), kv_bf.reshape(batch, seq_len, cw),
            _block_onehot(seq_len, nblk))
    return out.reshape(batch * seq_len, qd)


N_ROWS = N_HEADS * SUBLANES


def _block_diag_queries(q, n_kv, qbd_ref):
    hpg = N_HEADS // n_kv
    qf = q.astype(F32)
    qbd_ref[...] = jnp.zeros_like(qbd_ref)
    for h in range(N_HEADS):
        g = h // hpg
        qbd_ref[h * SUBLANES:(h + 1) * SUBLANES, g * HEAD_DIM:(g + 1) * HEAD_DIM] = \
            qf[:, h * HEAD_DIM:(h + 1) * HEAD_DIM]
    return qbd_ref[...].astype(BF16)


def _pick_group(res, n_kv):
    hpg = N_HEADS // n_kv
    rg = lax.broadcasted_iota(jnp.int32, (N_ROWS, 1), 0) // (hpg * SUBLANES)
    out = res[:, 0:HEAD_DIM]
    for g in range(1, n_kv):
        out = jnp.where(rg == g, res[:, g * HEAD_DIM:(g + 1) * HEAD_DIM], out)
    return out


def _osm_step_bd(carry, s, mask, v_all, n_kv):
    m, l, acc = carry
    if mask is not None:
        s = jnp.where(mask, s, NEG)
    m_new = jnp.maximum(m, jnp.max(s, axis=-1, keepdims=True))
    p = jnp.exp(s - m_new)
    if mask is not None:
        p = jnp.where(mask, p, 0.0)
    alpha = jnp.exp(m - m_new)
    l = alpha * l + jnp.sum(p, axis=-1, keepdims=True)
    acc = alpha * acc + _pick_group(_dot(p.astype(BF16), v_all), n_kv)
    return m_new, l, acc


def _pad_rows(x, rows):
    x = x.astype(F32)
    return jnp.concatenate([x, jnp.zeros((rows - x.shape[0], x.shape[1]), F32)], axis=0)


def _token_of_row():
    return lax.broadcasted_iota(jnp.int32, (N_ROWS, 1), 0) % SUBLANES


def _store_rows(o_ref, out, base):
    for h in range(N_HEADS):
        o_ref[base:base + SUBLANES, h * HEAD_DIM:(h + 1) * HEAD_DIM] = \
            out[h * SUBLANES:(h + 1) * SUBLANES].astype(o_ref.dtype)


def _new_token_step(carry, qbd, kn, vn, n_kv, bias=None):
    k_all = _pad_rows(kn, LANES).astype(BF16)
    v_all = _pad_rows(vn, LANES).astype(BF16)
    s = _dot_nt(qbd, k_all)
    if bias is not None:
        s = s + bias
    lane = lax.broadcasted_iota(jnp.int32, (1, LANES), 1)
    return _osm_step_bd(carry, s, lane <= _token_of_row(), v_all, n_kv)


def _swa_sample_kernel(sink_ref, q_ref, buf_ref, kvn_ref, o_ref, qbd_ref, *, window):
    n_seq, wb, _ = buf_ref.shape
    w = SWA_KV * HEAD_DIM
    tok = _token_of_row()
    hrow = lax.broadcasted_iota(jnp.int32, (N_ROWS, 1), 0) // SUBLANES
    sink = jnp.zeros((N_ROWS, 1), F32)
    for h in range(N_HEADS):
        sink = jnp.where(hrow == h, sink_ref[h], sink)
    lane = lax.broadcasted_iota(jnp.int32, (1, wb), 1)
    for s_i in range(n_seq):
        rows = slice(s_i * SUBLANES, (s_i + 1) * SUBLANES)
        qbd = _block_diag_queries(q_ref[rows, :], SWA_KV, qbd_ref)
        buf = buf_ref[s_i]
        carry = _osm_init(N_ROWS)
        s1 = _dot_nt(qbd, buf[:, 0:w].astype(BF16))
        carry = _osm_step_bd(carry, s1, lane >= tok + (wb - window), buf[:, w:2 * w].astype(BF16), SWA_KV)
        kvn = kvn_ref[rows, :]
        m, l, acc = _new_token_step(carry, qbd, kvn[:, 0:w], kvn[:, w:2 * w], SWA_KV)
        m_f = jnp.maximum(m, sink)
        scale = jnp.exp(m - m_f)
        out = acc * scale / (l * scale + jnp.exp(sink - m_f))
        _store_rows(o_ref, out, s_i * SUBLANES)


def _swa_sample_attention(q, buf, kvn_bf, sink):
    n_seq, wb, cw = buf.shape
    sb = min(8, n_seq)
    qd = q.shape[1]
    assert wb >= SWA_WIN and wb == LANES
    return pl.pallas_call(
        functools.partial(_swa_sample_kernel, window=SWA_WIN), grid=(n_seq // sb,),
        in_specs=[pl.BlockSpec(memory_space=pltpu.SMEM),
                  pl.BlockSpec((sb * SUBLANES, qd), lambda i: (i, 0)),
                  pl.BlockSpec((sb, wb, cw), lambda i: (i, 0, 0)),
                  pl.BlockSpec((sb * SUBLANES, cw), lambda i: (i, 0))],
        out_specs=pl.BlockSpec((sb * SUBLANES, qd), lambda i: (i, 0)),
        out_shape=jax.ShapeDtypeStruct(q.shape, BF16),
        scratch_shapes=[pltpu.VMEM((N_ROWS, SWA_KV * HEAD_DIM), F32)],
        compiler_params=_cparams(("parallel",)), name="swa_sample")(sink, q, buf, kvn_bf)


def _page_specs(n, block, layer, col_block, n_pages, per_step, last_chunk=None):
    def spec(k):
        def index_map(s, c, pt):
            cc = c if last_chunk is None else jnp.minimum(c, last_chunk)
            return (layer, pt[s * n_pages + cc * per_step + k], 0, col_block)
        return pl.BlockSpec((None, None) + block, index_map)

    return [spec(k) for k in range(n)]


def _fox_bias_kernel(pt_ref, *refs, ppc):
    page_refs = refs[:ppc]
    lfn_ref, dt_ref, dn_ref, carry_ref = refs[ppc:]
    c = pl.program_id(1)
    page = page_refs[0].shape[0]

    @pl.when(c == 0)
    def _():
        carry_ref[...] = jnp.zeros_like(carry_ref)

    eye = jnp.where(lax.broadcasted_iota(jnp.int32, (N_HEADS, N_HEADS), 0)
                    == lax.broadcasted_iota(jnp.int32, (N_HEADS, N_HEADS), 1), 1.0, 0.0).astype(F32)
    tri = _tri(page, True)
    carry = carry_ref[...]
    for k in range(ppc):
        lft = _dot_nt(eye, page_refs[k][...], precision=HIGHEST)
        d = carry + jnp.dot(lft, tri, precision=HIGHEST, preferred_element_type=F32)
        dt_ref[0, :, k * page:(k + 1) * page] = d
        carry = d[:, page - 1:page]
    carry_ref[...] = carry

    @pl.when(c == pl.num_programs(1) - 1)
    def _():
        dn_ref[0] = carry + jnp.dot(lfn_ref[0], _tri(LANES, True), precision=HIGHEST,
                                    preferred_element_type=F32)


def _fox_bias(cache_logf, layer, pt_flat, n_seq, n_pages, lf_new_t):
    page = cache_logf.shape[2]
    ppc = min(8, n_pages)
    assert page == LANES and n_pages % ppc == 0
    lfn = jnp.pad(lf_new_t, ((0, 0), (0, 0), (0, LANES - lf_new_t.shape[2])))
    grid_spec = pltpu.PrefetchScalarGridSpec(
        num_scalar_prefetch=1, grid=(n_seq, n_pages // ppc),
        in_specs=_page_specs(ppc, (page, N_HEADS), layer, 0, n_pages, ppc)
        + [pl.BlockSpec((1, N_HEADS, LANES), lambda s, c, pt: (s, 0, 0))],
        out_specs=[pl.BlockSpec((1, N_HEADS, ppc * page), lambda s, c, pt: (s, 0, c)),
                   pl.BlockSpec((1, N_HEADS, LANES), lambda s, c, pt: (s, 0, 0))],
        scratch_shapes=[pltpu.VMEM((N_HEADS, 1), F32)])
    return pl.pallas_call(
        functools.partial(_fox_bias_kernel, ppc=ppc), grid_spec=grid_spec,
        out_shape=[jax.ShapeDtypeStruct((n_seq, N_HEADS, n_pages * page), F32),
                   jax.ShapeDtypeStruct((n_seq, N_HEADS, LANES), F32)],
        compiler_params=_cparams(("parallel", "arbitrary")), name="fox_bias")(
            pt_flat, *([cache_logf] * ppc), lfn)


def _rows_from_heads(x):
    return jnp.broadcast_to(x[:, None, :], (N_HEADS, SUBLANES, x.shape[1])).reshape(N_ROWS, x.shape[1])


def _fox_sample_kernel(pt_ref, *refs, ppc):
    page_refs = refs[:ppc]
    q_ref, dt_ref, dq_ref, dn_ref, kvn_ref, o_ref, qbd_ref, m_ref, l_ref, acc_ref, kbuf_ref = refs[ppc:]
    c = pl.program_id(1)
    n_chunks = pl.num_programs(1) - 1
    page = page_refs[0].shape[0]
    w = FOX_KV * HEAD_DIM

    @pl.when(c == 0)
    def _():
        qbd_ref[...] = _block_diag_queries(q_ref[...], FOX_KV, qbd_ref).astype(F32)
        m_ref[...], l_ref[...], acc_ref[...] = _osm_init(N_ROWS)

    qbd = qbd_ref[...].astype(BF16)
    dq = dq_ref[0]

    @pl.when(c < n_chunks)
    def _():
        for k in range(ppc):
            kbuf_ref[k * page:(k + 1) * page, :] = page_refs[k][...].astype(BF16)
        s = _dot_nt(qbd, kbuf_ref[:, 0:w]) + (dq - _rows_from_heads(dt_ref[0]))
        carry = _osm_step_bd((m_ref[...], l_ref[...], acc_ref[...]), s, None, kbuf_ref[:, w:2 * w], FOX_KV)
        m_ref[...], l_ref[...], acc_ref[...] = carry

    @pl.when(c == n_chunks)
    def _():
        kvn = kvn_ref[...]
        carry = _new_token_step((m_ref[...], l_ref[...], acc_ref[...]), qbd, kvn[:, 0:w], kvn[:, w:2 * w],
                                FOX_KV, bias=dq - _rows_from_heads(dn_ref[0]))
        _store_rows(o_ref, _osm_finish(carry), 0)


def _fox_sample_attention(q, cache_kv, layer, pt_flat, n_seq, n_pages, d_past, d_new, kvn_bf):
    page, cw = cache_kv.shape[2], cache_kv.shape[3]
    ppc = min(8, n_pages)
    n_chunks = n_pages // ppc
    qd = q.shape[1]
    dq = d_new[:, :, :SUBLANES].reshape(n_seq, N_ROWS, 1)
    last = n_chunks - 1
    grid_spec = pltpu.PrefetchScalarGridSpec(
        num_scalar_prefetch=1, grid=(n_seq, n_chunks + 1),
        in_specs=_page_specs(ppc, (page, cw), layer, 0, n_pages, ppc, last_chunk=last)
        + [pl.BlockSpec((SUBLANES, qd), lambda s, c, pt: (s, 0)),
           pl.BlockSpec((1, N_HEADS, ppc * page), lambda s, c, pt: (s, 0, jnp.minimum(c, last))),
           pl.BlockSpec((1, N_ROWS, 1), lambda s, c, pt: (s, 0, 0)),
           pl.BlockSpec((1, N_HEADS, LANES), lambda s, c, pt: (s, 0, 0)),
           pl.BlockSpec((SUBLANES, cw), lambda s, c, pt: (s, 0))],
        out_specs=pl.BlockSpec((SUBLANES, qd), lambda s, c, pt: (s, 0)),
        scratch_shapes=[pltpu.VMEM((N_ROWS, FOX_KV * HEAD_DIM), F32),
                        pltpu.VMEM((N_ROWS, 1), F32), pltpu.VMEM((N_ROWS, 1), F32),
                        pltpu.VMEM((N_ROWS, HEAD_DIM), F32),
                        pltpu.VMEM((ppc * page, cw), BF16)])
    return pl.pallas_call(
        functools.partial(_fox_sample_kernel, ppc=ppc), grid_spec=grid_spec,
        out_shape=jax.ShapeDtypeStruct(q.shape, BF16),
        compiler_params=_cparams(("parallel", "arbitrary")), name="fox_sample")(
            pt_flat, *([cache_kv] * ppc), q, d_past, dq, d_new, kvn_bf)


def _compress_paged_kernel(pt_ref, *refs, ppc):
    page_refs = refs[:ppc]
    w_ref, o_ref = refs[ppc:]
    page, c = page_refs[0].shape
    parts = []
    for k in range(ppc):
        x = (page_refs[k][...] * w_ref[...]).reshape(page // NSA_BLK, NSA_BLK, c)
        parts.append(jnp.sum(x, axis=1))
    o_ref[0] = jnp.concatenate(parts, axis=0)


def _compress_paged(cache, layer, pt_flat, n_seq, n_pages, w_cmp):
    page = cache.shape[2]
    c = 2 * NSA_KV * HEAD_DIM
    per_page = page // NSA_BLK
    ppc = SUBLANES // per_page
    assert n_pages % ppc == 0
    w = jnp.tile(w_cmp.reshape(NSA_BLK, c), (per_page, 1))
    grid_spec = pltpu.PrefetchScalarGridSpec(
        num_scalar_prefetch=1, grid=(n_seq, n_pages // ppc),
        in_specs=_page_specs(ppc, (page, c), layer, 0, n_pages, ppc)
        + [pl.BlockSpec((page, c), lambda s, cc, pt: (0, 0))],
        out_specs=pl.BlockSpec((1, SUBLANES, c), lambda s, cc, pt: (s, cc, 0)))
    return pl.pallas_call(
        functools.partial(_compress_paged_kernel, ppc=ppc), grid_spec=grid_spec,
        out_shape=jax.ShapeDtypeStruct((n_seq, n_pages * per_page, c), F32),
        compiler_params=_cparams(("parallel", "parallel")), name="nsa_compress_paged")(
            pt_flat, *([cache] * ppc), w)


def _nsa_sample_kernel(pt_ref, *refs, ppc, past):
    page_refs = refs[:ppc]
    (q_ref, gate_ref, kvc_ref, onehot_ref, win_ref, kvn_ref, o_ref,
     qbd_ref, sel_ref, ocmp_ref, m_ref, l_ref, acc_ref, kbuf_ref) = refs[ppc:]
    c = pl.program_id(1)
    n_chunks = pl.num_programs(1) - 1
    page = page_refs[0].shape[0]
    w = NSA_KV * HEAD_DIM
    nblk = kvc_ref.shape[1]
    tok = _token_of_row()

    @pl.when(c == 0)
    def _():
        qbd0 = _block_diag_queries(q_ref[...], NSA_KV, qbd_ref)
        kc_hi, kc_lo = _split_bf16(kvc_ref[0, :, 0:w])
        s_c = _dot_nt(qbd0, kc_hi) + _dot_nt(qbd0, kc_lo)
        blk = lax.broadcasted_iota(jnp.int32, (1, nblk), 1)
        p_c = _softmax_rows(s_c, (blk + 1) * NSA_BLK - 1 <= past + tok)
        ocmp_ref[...] = _pick_group(_dot(p_c.astype(BF16), kvc_ref[0, :, w:2 * w].astype(BF16)), NSA_KV)
        imp = jnp.sum(p_c.reshape(NSA_KV, NSA_HPG, SUBLANES, nblk), axis=1)
        imp = imp.reshape(NSA_KV * SUBLANES, nblk)
        cur = (past + lax.broadcasted_iota(jnp.int32, (NSA_KV * SUBLANES, 1), 0) % SUBLANES) // NSA_BLK
        sel = _select_blocks(imp, cur, NSA_TOPK - NSA_N_FORCED)
        sel = jnp.broadcast_to(sel.reshape(NSA_KV, 1, SUBLANES, nblk), (NSA_KV, NSA_HPG, SUBLANES, nblk))
        sel_ref[...] = sel.reshape(N_ROWS, nblk)
        m_ref[...], l_ref[...], acc_ref[...] = _osm_init(N_ROWS)

    qbd = qbd_ref[...].astype(BF16)

    @pl.when(c < n_chunks)
    def _():
        for k in range(ppc):
            kbuf_ref[k * page:(k + 1) * page, :] = page_refs[k][...].astype(BF16)
        s = _dot_nt(qbd, kbuf_ref[:, 0:w])
        picked = _dot_nt(sel_ref[...].astype(BF16), onehot_ref[...]) > 0.5
        carry = _osm_step_bd((m_ref[...], l_ref[...], acc_ref[...]), s, picked, kbuf_ref[:, w:2 * w], NSA_KV)
        m_ref[...], l_ref[...], acc_ref[...] = carry

    @pl.when(c == n_chunks)
    def _():
        kvn = kvn_ref[...]
        o_slc = _osm_finish(_new_token_step((m_ref[...], l_ref[...], acc_ref[...]), qbd,
                                            kvn[:, 0:w], kvn[:, w:2 * w], NSA_KV))
        buf = win_ref[0]
        wb = buf.shape[0]
        lane = lax.broadcasted_iota(jnp.int32, (1, wb), 1)
        carry = _osm_step_bd(_osm_init(N_ROWS), _dot_nt(qbd, buf[:, 0:w].astype(BF16)),
                             lane >= tok + (wb - NSA_WIN), buf[:, w:2 * w].astype(BF16), NSA_KV)
        o_win = _osm_finish(_new_token_step(carry, qbd, kvn[:, 2 * w:3 * w], kvn[:, 3 * w:4 * w], NSA_KV))
        gates = gate_ref[...]
        out = (ocmp_ref[...] * _head_columns(gates, 0, N_HEADS)
               + o_slc * _head_columns(gates, N_HEADS, N_HEADS)
               + o_win * _head_columns(gates, 2 * N_HEADS, N_HEADS))
        _store_rows(o_ref, out, 0)


def _nsa_sample_attention(q, gates, kvc, cache, layer, pt_flat, n_pages, win_buf, kvn_bf):
    n_seq, nblk, _ = kvc.shape
    page = cache.shape[2]
    past = n_pages * page
    ppc = min(8, n_pages)
    n_chunks = n_pages // ppc
    last = n_chunks - 1
    qd = q.shape[1]
    wb = win_buf.shape[1]
    c = 2 * NSA_KV * HEAD_DIM
    assert past % NSA_BLK == 0 and wb >= NSA_WIN and nblk == past // NSA_BLK and nblk >= NSA_TOPK
    grid_spec = pltpu.PrefetchScalarGridSpec(
        num_scalar_prefetch=1, grid=(n_seq, n_chunks + 1),
        in_specs=_page_specs(ppc, (page, c), layer, 1, n_pages, ppc, last_chunk=last)
        + [pl.BlockSpec((SUBLANES, qd), lambda s, cc, pt: (s, 0)),
           pl.BlockSpec((SUBLANES, LANES), lambda s, cc, pt: (s, 0)),
           pl.BlockSpec((1, nblk, c), lambda s, cc, pt: (s, 0, 0)),
           pl.BlockSpec((ppc * page, nblk), lambda s, cc, pt: (jnp.minimum(cc, last), 0)),
           pl.BlockSpec((1, wb, c), lambda s, cc, pt: (s, 0, 0)),
           pl.BlockSpec((SUBLANES, 2 * c), lambda s, cc, pt: (s, 0))],
        out_specs=pl.BlockSpec((SUBLANES, qd), lambda s, cc, pt: (s, 0)),
        scratch_shapes=[pltpu.VMEM((N_ROWS, NSA_KV * HEAD_DIM), F32),
                        pltpu.VMEM((N_ROWS, nblk), F32),
                        pltpu.VMEM((N_ROWS, HEAD_DIM), F32),
                        pltpu.VMEM((N_ROWS, 1), F32), pltpu.VMEM((N_ROWS, 1), F32),
                        pltpu.VMEM((N_ROWS, HEAD_DIM), F32),
                        pltpu.VMEM((ppc * page, c), BF16)])
    return pl.pallas_call(
        functools.partial(_nsa_sample_kernel, ppc=ppc, past=past), grid_spec=grid_spec,
        out_shape=jax.ShapeDtypeStruct(q.shape, BF16),
        compiler_params=_cparams(("parallel", "arbitrary")), name="nsa_sample")(
            pt_flat, *([cache] * ppc), q, gates, kvc, _block_onehot(past, nblk), win_buf, kvn_bf)


def _pad_cols(w, cols):
    return jnp.pad(w, ((0, 0), (0, cols - w.shape[1])))


def kernel(x_prompt, x_sample, page_table, cache_nsa_kv, state_nsa_win, state_swa_kv, cache_fox_kv,
           cache_fox_logf, state_ffn_conv, norm_mix, norm_ffn, norm_final, nsa_wq, nsa_wkv, nsa_cmp_w,
           nsa_wg, nsa_bg, nsa_wo, swa_wq, swa_wkv, swa_sink, swa_wo, fox_wq, fox_wkv, fox_wf, fox_bf,
           fox_wo, ffn_w_in, ffn_conv_w, ffn_conv_b, ffn_w_down):
    batch, seq, d = x_prompt.shape
    n_seq, dec, _ = x_sample.shape
    n_pages = page_table.shape[1]
    page = cache_nsa_kv.shape[2]
    past = n_pages * page
    depth = norm_mix.shape[0]
    assert dec == SUBLANES
    xp = x_prompt.reshape(batch * seq, d)
    xs = x_sample.reshape(n_seq * dec, d)
    pos_p = jnp.arange(seq)
    pos_s = past + jnp.arange(dec)
    pt_flat = page_table.reshape(-1)
    nsa_cache = cache_nsa_kv.reshape(cache_nsa_kv.shape[:3] + (-1,))
    fox_cache = cache_fox_kv.reshape(cache_fox_kv.shape[:3] + (-1,))

    outs = {k: [] for k in ("nsa_kv_p", "nsa_kv_s", "nsa_win_p", "nsa_win_s", "swa_p", "swa_s",
                            "fox_kv_p", "fox_kv_s", "fox_lf_p", "fox_lf_s", "conv_p", "conv_s")}
    for i in range(depth):
        j, kind = i // N_MIXERS, i % N_MIXERS
        g_mix = norm_mix[i]
        if kind == 0:
            w_cat = jnp.concatenate([nsa_wq[j], nsa_wkv[j], _pad_cols(nsa_wg[j], LANES)], axis=1).astype(BF16)
            extras = {"bg": _pad_cols(nsa_bg[j].reshape(1, -1), LANES)}
            q, kv, win, kvbf, gates = _project(xp, seq, pos_p, g_mix, w_cat, "nsa", extras)
            kvc = _compress(kv, nsa_cmp_w[j])
            o_p = _nsa_attention(q, gates, kvc, kvbf, batch, seq)
            outs["nsa_kv_p"].append(kv.reshape(batch, seq, 4, NSA_KV, HEAD_DIM))
            outs["nsa_win_p"].append(win.reshape(batch, seq, 2, NSA_KV, HEAD_DIM)[:, seq - min(NSA_WIN, seq):])

            q, kv, win, kvbf, gates = _project(xs, dec, pos_s, g_mix, w_cat, "nsa", extras)
            kvc = _compress_paged(nsa_cache, j, pt_flat, n_seq, n_pages, nsa_cmp_w[j])
            win_buf = state_nsa_win[j]
            wb = win_buf.shape[1]
            o_s = _nsa_sample_attention(q, gates, kvc, nsa_cache, j, pt_flat, n_pages,
                                        win_buf.reshape(n_seq, wb, -1), kvbf)
            outs["nsa_kv_s"].append(kv.reshape(n_seq, dec, 4, NSA_KV, HEAD_DIM))
            win_new = win.reshape(n_seq, dec, 2, NSA_KV, HEAD_DIM)
            outs["nsa_win_s"].append(jnp.concatenate([win_buf, win_new], axis=1)[:, -wb:])
            wo = nsa_wo[j]
        elif kind == 1:
            w_cat = jnp.concatenate([swa_wq[j], swa_wkv[j]], axis=1).astype(BF16)
            q, kv, kvbf = _project(xp, seq, pos_p, g_mix, w_cat, "swa", None)
            o_p = _band_attention(q, kvbf, swa_sink[j], batch, seq, SWA_WIN, SWA_KV, 0, 1)
            outs["swa_p"].append(kv.reshape(batch, seq, 2, SWA_KV, HEAD_DIM)[:, seq - min(SWA_WIN, seq):])

            q, kv, kvbf = _project(xs, dec, pos_s, g_mix, w_cat, "swa", None)
            buf = state_swa_kv[j]
            wb = buf.shape[1]
            o_s = _swa_sample_attention(q, buf.reshape(n_seq, wb, -1), kvbf, swa_sink[j])
            kv_new = kv.reshape(n_seq, dec, 2, SWA_KV, HEAD_DIM)
            outs["swa_s"].append(jnp.concatenate([buf, kv_new], axis=1)[:, -wb:])
            wo = swa_wo[j]
        else:
            w_cat = jnp.concatenate([fox_wq[j], fox_wkv[j]], axis=1).astype(BF16)
            extras = {"wf": fox_wf[j], "bf": fox_bf[j]}
            q, kv, kvbf, lf, lft = _project(xp, seq, pos_p, g_mix, w_cat, "fox", extras)
            tm = lft.shape[2]
            lft = lft.reshape(batch, seq // tm, N_HEADS, tm).transpose(0, 2, 1, 3).reshape(batch, N_HEADS, seq)
            o_p = _fox_attention(q, kvbf, lf, lft, batch, seq)
            outs["fox_kv_p"].append(kv.reshape(batch, seq, 2, FOX_KV, HEAD_DIM))
            outs["fox_lf_p"].append(lf.reshape(batch, seq, N_HEADS))

            q, kv, kvbf, lf, lft = _project(xs, dec, pos_s, g_mix, w_cat, "fox", extras)
            tm = lft.shape[2]
            lft = lft.reshape(-1, N_HEADS, tm // dec, dec).transpose(0, 2, 1, 3).reshape(n_seq, N_HEADS, dec)
            d_past, d_new = _fox_bias(cache_fox_logf, j, pt_flat, n_seq, n_pages, lft)
            o_s = _fox_sample_attention(q, fox_cache, j, pt_flat, n_seq, n_pages, d_past, d_new, kvbf)
            outs["fox_kv_s"].append(kv.reshape(n_seq, dec, 2, FOX_KV, HEAD_DIM))
            outs["fox_lf_s"].append(lf.reshape(n_seq, dec, N_HEADS))
            wo = fox_wo[j]

        wo_bf, w_in_bf, w_down_bf = wo.astype(BF16), ffn_w_in[i].astype(BF16), ffn_w_down[i].astype(BF16)
        xp, conv_p = _post(xp, o_p, seq, wo_bf, norm_ffn[i], w_in_bf, ffn_conv_w[i], ffn_conv_b[i], w_down_bf, None)
        xs, conv_s = _post(xs, o_s, dec, wo_bf, norm_ffn[i], w_in_bf, ffn_conv_w[i], ffn_conv_b[i], w_down_bf,
                           state_ffn_conv[i])
        outs["conv_p"].append(conv_p)
        outs["conv_s"].append(conv_s)

    y_prompt = _final_norm(xp, norm_final).reshape(batch, seq, d)
    y_sample = _final_norm(xs, norm_final).reshape(n_seq, dec, d)
    st = lambda k: jnp.stack(outs[k])
    return (y_prompt, y_sample, st("nsa_kv_p"), st("nsa_kv_s"), st("nsa_win_p"), st("nsa_win_s"),
            st("swa_p"), st("swa_s"), st("fox_kv_p"), st("fox_kv_s"), st("fox_lf_p"), st("fox_lf_s"),
            st("conv_p"), st("conv_s"))
```

```python
import functools
import math

import numpy as np
import jax
import jax.numpy as jnp
from jax import lax
from jax.experimental import pallas as pl
from jax.experimental.pallas import tpu as pltpu

F32 = jnp.float32
BF16 = jnp.bfloat16
HIGHEST = lax.Precision.HIGHEST

HEAD_DIM = 64
N_HEADS = 16
ATTN_SCALE = HEAD_DIM ** -0.5
LOG2E = math.log2(math.e)
ROPE_THETA = 10000.0
RMS_EPS = 1e-6
N_MIXERS = 3
NSA_KV = 2
NSA_HPG = N_HEADS // NSA_KV
NSA_BLK = 64
NSA_TOPK = 16
NSA_N_FORCED = 3
NSA_WIN = 512
SWA_KV = 2
SWA_WIN = 128
FOX_KV = 4
FOX_HPG = N_HEADS // FOX_KV
CONV_W = 3
LANES = 128
SUBLANES = 8
NEG = -1e30
UNSELECTED = -32768.0
VT_ROWS = 80
VMEM_LIMIT = 56 * 1024 * 1024


def _cparams(sem):
    return pltpu.CompilerParams(dimension_semantics=sem, vmem_limit_bytes=VMEM_LIMIT)


def _resident(shape, index_map):
    return pl.BlockSpec(shape, index_map, pipeline_mode=pl.Buffered(1))


def _const_spec(a):
    nd = a.ndim
    return _resident(a.shape, lambda *_: (0,) * nd)


def _dot(a, b, precision=None):
    return jnp.dot(a, b, precision=precision, preferred_element_type=F32)


def _dot_nt(a, b, precision=None):
    return lax.dot_general(a, b, (((1,), (1,)), ((), ())), precision=precision,
                           preferred_element_type=F32)


def _rms(x, g):
    ms = jnp.mean(x * x, axis=-1, keepdims=True)
    return x * lax.rsqrt(ms + RMS_EPS) * g


def _rope_chunk(y, cos, sin_signed, first_half):
    swapped = jnp.where(first_half, pltpu.roll(y, LANES - 32, 1), pltpu.roll(y, 32, 1))
    return y * cos + swapped * sin_signed


def _log_sigmoid(z):
    return jnp.minimum(z, 0.0) - jnp.log(1.0 + jnp.exp(-jnp.abs(z)))


def _eye(n, dtype):
    r = lax.broadcasted_iota(jnp.int32, (n, n), 0)
    c = lax.broadcasted_iota(jnp.int32, (n, n), 1)
    return jnp.where(r == c, 1.0, 0.0).astype(dtype)


def _tri(n, upper):
    r = lax.broadcasted_iota(jnp.int32, (n, n), 0)
    c = lax.broadcasted_iota(jnp.int32, (n, n), 1)
    return jnp.where((r <= c) if upper else (r >= c), 1.0, 0.0).astype(F32)


def _split_bf16(x):
    hi = x.astype(BF16)
    return hi, (x - hi.astype(F32)).astype(BF16)


def _split3_bf16(x):
    hi = x.astype(BF16)
    r = x - hi.astype(F32)
    mid = r.astype(BF16)
    return hi, mid, (r - mid.astype(F32)).astype(BF16)


def _proj_kernel(*refs, mode, has_rope, q_scale):
    it = iter(refs)
    x_ref, g_ref, w_ref = next(it), next(it), next(it)
    if has_rope:
        cos_ref, sin_ref = next(it), next(it)
    if mode == "nsa":
        bg_ref, wgt_ref, bgt_ref = next(it), next(it), next(it)
        q_out, kv_out, win_out, kvbf_out, gate_out, gatet_out = (next(it) for _ in range(6))
    elif mode == "swa":
        q_out, kv_out, kvbf_out = (next(it) for _ in range(3))
    else:
        wf_ref, wft_ref, bf_ref, bft_ref = (next(it) for _ in range(4))
        q_out, kv_out, kvbf_out, lf_out, lft_out = (next(it) for _ in range(5))

    hn = _rms(x_ref[...], g_ref[...])
    hb = hn.astype(BF16)
    y = _dot(hb, w_ref[...])
    if has_rope:
        cos, sin = cos_ref[...], sin_ref[...]
        first_half = (lax.broadcasted_iota(jnp.int32, cos.shape, 1) % HEAD_DIM) < HEAD_DIM // 2

    def chunk(c, rope):
        yc = y[:, c * LANES:(c + 1) * LANES]
        return _rope_chunk(yc, cos, sin, first_half) if rope else yc

    n_q = N_HEADS * HEAD_DIM // LANES
    for c in range(n_q):
        q_out[:, c * LANES:(c + 1) * LANES] = (chunk(c, has_rope) * q_scale).astype(BF16)
    if mode == "nsa":
        for c in range(6):
            yc = chunk(n_q + c, c % 2 == 0)
            if c < 4:
                kv_out[:, c * LANES:(c + 1) * LANES] = yc
            else:
                win_out[:, (c - 4) * LANES:(c - 3) * LANES] = yc
            if c >= 2:
                kvbf_out[:, (c - 2) * LANES:(c - 1) * LANES] = yc.astype(BF16)
        gate_out[...] = jax.nn.sigmoid(chunk(n_q + 6, False) + bg_ref[...])
        gatet_out[0] = jax.nn.sigmoid(_dot_nt(wgt_ref[...], hb) + bgt_ref[...])
    elif mode == "swa":
        for c in range(2):
            yc = chunk(n_q + c, c == 0)
            kv_out[:, c * LANES:(c + 1) * LANES] = yc
            kvbf_out[:, c * LANES:(c + 1) * LANES] = yc.astype(BF16)
    else:
        for c in range(4):
            yc = chunk(n_q + c, False)
            kv_out[:, c * LANES:(c + 1) * LANES] = yc
            kvbf_out[:, c * LANES:(c + 1) * LANES] = yc.astype(BF16)
        lf_out[...] = _log_sigmoid(_dot(hn, wf_ref[...], precision=HIGHEST) + bf_ref[...])
        lft_out[0] = _log_sigmoid(_dot_nt(wft_ref[...], hn, precision=HIGHEST) + bft_ref[...])


def _rope_tables(pos):
    half = HEAD_DIM // 2
    inv_freq = jnp.exp(-math.log(ROPE_THETA) * jnp.arange(half, dtype=F32) / half)
    ang = pos.astype(F32)[:, None] * inv_freq[None, :]
    cos, sin = jnp.cos(ang), jnp.sin(ang)
    return jnp.tile(cos, (1, 4)), jnp.concatenate([-sin, sin, -sin, sin], axis=1)


def _project(x, seq_len, pos, g, w_cat, mode, extras, q_scale):
    n, d = x.shape
    tm = min(512, n)
    assert n % tm == 0 and (seq_len % tm == 0 or tm % seq_len == 0)
    has_rope = mode != "fox"
    c_tot = w_cat.shape[1]
    in_specs = [pl.BlockSpec((tm, d), lambda i: (i, 0)),
                _resident((1, d), lambda i: (0, 0)),
                _resident((d, c_tot), lambda i: (0, 0))]
    args = [x, g.reshape(1, d), w_cat]
    if has_rope:
        cos, sin = _rope_tables(pos)
        if seq_len < tm:
            cos, sin = jnp.tile(cos, (tm // seq_len, 1)), jnp.tile(sin, (tm // seq_len, 1))
        period = cos.shape[0] // tm
        in_specs += [pl.BlockSpec((tm, LANES), lambda i: (i % period, 0))] * 2
        args += [cos, sin]
    qd = N_HEADS * HEAD_DIM
    row = lambda w: pl.BlockSpec((tm, w), lambda i: (i, 0))
    col = lambda h: pl.BlockSpec((1, h, tm), lambda i: (i, 0, 0))
    col_shape = lambda h: jax.ShapeDtypeStruct((n // tm, h, tm), F32)
    if mode == "nsa":
        bg, wg_t = extras["bg"], extras["wg_t"]
        in_specs += [_const_spec(bg), _const_spec(wg_t), _resident((LANES, 1), lambda i: (0, 0))]
        args += [bg, wg_t, bg.reshape(LANES, 1)]
        widths = [(qd, BF16), (512, F32), (256, F32), (512, BF16), (LANES, F32)]
        out_specs = [row(w) for w, _ in widths] + [col(LANES)]
        out_shape = [jax.ShapeDtypeStruct((n, w), dt) for w, dt in widths] + [col_shape(LANES)]
    elif mode == "swa":
        widths = [(qd, BF16), (256, F32), (256, BF16)]
        out_specs = [row(w) for w, _ in widths]
        out_shape = [jax.ShapeDtypeStruct((n, w), dt) for w, dt in widths]
    else:
        wf, bf = extras["wf"], extras["bf"]
        in_specs += [_resident((d, N_HEADS), lambda i: (0, 0)),
                     _resident((N_HEADS, d), lambda i: (0, 0)),
                     _resident((1, N_HEADS), lambda i: (0, 0)),
                     _resident((N_HEADS, 1), lambda i: (0, 0))]
        args += [wf, wf.T, bf.reshape(1, N_HEADS), bf.reshape(N_HEADS, 1)]
        widths = [(qd, BF16), (512, F32), (512, BF16), (N_HEADS, F32)]
        out_specs = [row(w) for w, _ in widths] + [col(N_HEADS)]
        out_shape = [jax.ShapeDtypeStruct((n, w), dt) for w, dt in widths] + [col_shape(N_HEADS)]
    return pl.pallas_call(
        functools.partial(_proj_kernel, mode=mode, has_rope=has_rope, q_scale=q_scale),
        grid=(n // tm,), in_specs=in_specs, out_specs=out_specs, out_shape=out_shape,
        compiler_params=_cparams(("parallel",)), name=f"proj_{mode}")(*args)


def _untile_cols(xt, batch, seq_len):
    tiles, h, tm = xt.shape
    if seq_len >= tm:
        return xt.reshape(batch, seq_len // tm, h, tm).transpose(0, 2, 1, 3).reshape(batch, h, seq_len)
    return xt.reshape(tiles, h, tm // seq_len, seq_len).transpose(0, 2, 1, 3).reshape(batch, h, seq_len)


def _post_kernel(*refs, per_seq_tiles, grouped, n_chunks):
    it = iter(refs)
    x_ref, o_ref, wo_ref, g_ref, win_ref, cw_ref, cb_ref, wd_ref = (next(it) for _ in range(8))
    if grouped:
        s0_ref, s1_ref = next(it), next(it)
    xo_ref, gtail_ref = next(it), next(it)
    if not grouped:
        carry_ref = next(it)

    tm = x_ref.shape[0]
    d_ff = wd_ref.shape[0]
    fc = d_ff // n_chunks
    x1 = x_ref[...] + _dot(o_ref[...], wo_ref[...])
    hb = _rms(x1, g_ref[...]).astype(BF16)
    row = lax.broadcasted_iota(jnp.int32, (tm, 1), 0)
    if grouped:
        r_in = row % SUBLANES
    else:
        r_in = row
        first_tile = pl.program_id(0) % per_seq_tiles == 0
    acc = jnp.zeros_like(x1)
    for c in range(n_chunks):
        cols = slice(c * fc, (c + 1) * fc)
        gch = _dot(hb, win_ref[:, c * fc:(c + 1) * fc])
        uch = _dot(hb, win_ref[:, d_ff + c * fc:d_ff + (c + 1) * fc])
        if grouped:
            prev1 = s1_ref[:, cols]
            prev2a, prev2b = s0_ref[:, cols], s1_ref[:, cols]
        else:
            prev2a = jnp.where(first_tile, 0.0, carry_ref[0:1, cols])
            prev1 = prev2b = jnp.where(first_tile, 0.0, carry_ref[1:2, cols])
        gm1 = jnp.where(r_in == 0, prev1, pltpu.roll(gch, 1, 0))
        gm2 = jnp.where(r_in == 0, prev2a, jnp.where(r_in == 1, prev2b, pltpu.roll(gch, 2, 0)))
        gc = cb_ref[:, cols] + cw_ref[0:1, cols] * gm2 + cw_ref[1:2, cols] * gm1 + cw_ref[2:3, cols] * gch
        a = gc * jax.nn.sigmoid(gc) * uch
        acc = acc + _dot(a.astype(BF16), wd_ref[c * fc:(c + 1) * fc, :])
        if grouped:
            gtail_ref[:, cols] = gch
        else:
            gtail_ref[0, :, cols] = gch[tm - SUBLANES:, :]
            carry_ref[0:2, cols] = gch[tm - 2:, :]
    xo_ref[...] = x1 + acc


def _post(x, o, seq_len, wo, g, w_in, conv_w, conv_b, w_down, conv_state):
    n, d = x.shape
    d_ff = w_down.shape[0]
    grouped = conv_state is not None
    tm = min(256 if grouped else 512, n)
    n_seq = n // seq_len
    in_specs = [pl.BlockSpec((tm, d), lambda i: (i, 0)),
                pl.BlockSpec((tm, o.shape[1]), lambda i: (i, 0)),
                _resident(wo.shape, lambda i: (0, 0)),
                _resident((1, d), lambda i: (0, 0)),
                _resident(w_in.shape, lambda i: (0, 0)),
                _resident((CONV_W, d_ff), lambda i: (0, 0)),
                _resident((1, d_ff), lambda i: (0, 0)),
                _resident(w_down.shape, lambda i: (0, 0))]
    args = [x, o, wo, g.reshape(1, d), w_in, conv_w, conv_b.reshape(1, d_ff), w_down]
    scratch = []
    if grouped:
        assert seq_len == SUBLANES
        in_specs += [pl.BlockSpec((tm, d_ff), lambda i: (i, 0))] * 2
        args += [jnp.repeat(conv_state[:, 0], seq_len, axis=0), jnp.repeat(conv_state[:, 1], seq_len, axis=0)]
        tail_spec = pl.BlockSpec((tm, d_ff), lambda i: (i, 0))
        tail_shape = jax.ShapeDtypeStruct((n, d_ff), F32)
        per_seq_tiles = 1
    else:
        assert seq_len % tm == 0
        per_seq_tiles = seq_len // tm
        tail_spec = pl.BlockSpec((1, SUBLANES, d_ff), lambda i: (i // per_seq_tiles, 0, 0))
        tail_shape = jax.ShapeDtypeStruct((n_seq, SUBLANES, d_ff), F32)
        scratch = [pltpu.VMEM((SUBLANES, d_ff), F32)]
    x_new, tail = pl.pallas_call(
        functools.partial(_post_kernel, per_seq_tiles=per_seq_tiles, grouped=grouped, n_chunks=2),
        grid=(n // tm,), in_specs=in_specs,
        out_specs=[pl.BlockSpec((tm, d), lambda i: (i, 0)), tail_spec],
        out_shape=[jax.ShapeDtypeStruct((n, d), F32), tail_shape],
        scratch_shapes=scratch,
        compiler_params=_cparams(("arbitrary",)), name="post_ffn")(*args)
    if grouped:
        conv_new = tail.reshape(n_seq, seq_len, d_ff)[:, seq_len - (CONV_W - 1):]
    else:
        conv_new = tail[:, SUBLANES - (CONV_W - 1):]
    return x_new, conv_new


def _final_norm_kernel(x_ref, g_ref, o_ref):
    o_ref[...] = _rms(x_ref[...], g_ref[...])


def _final_norm(x, g):
    n, d = x.shape
    tm = min(1024, n)
    return pl.pallas_call(
        _final_norm_kernel, grid=(n // tm,),
        in_specs=[pl.BlockSpec((tm, d), lambda i: (i, 0)), _resident((1, d), lambda i: (0, 0))],
        out_specs=pl.BlockSpec((tm, d), lambda i: (i, 0)),
        out_shape=jax.ShapeDtypeStruct((n, d), F32),
        compiler_params=_cparams(("parallel",)), name="final_norm")(x, g.reshape(1, d))


def _osm_init(rows):
    return (jnp.full((rows, 1), NEG, F32), jnp.zeros((rows, 1), F32), jnp.zeros((rows, HEAD_DIM), F32))


def _osm_step(carry, s, mask, v):
    m, l, acc = carry
    s = jnp.where(mask, s, NEG)
    m_new = jnp.maximum(m, jnp.max(s, axis=-1, keepdims=True))
    p = jnp.where(mask, jnp.exp(s - m_new), 0.0)
    alpha = jnp.exp(m - m_new)
    l = alpha * l + jnp.sum(p, axis=-1, keepdims=True)
    acc = alpha * acc + _dot(p.astype(BF16), v)
    return m_new, l, acc


def _osm_finish(carry):
    m, l, acc = carry
    return acc / jnp.maximum(l, 1e-30)


def _stack_heads(q, g, hpg):
    return jnp.concatenate(
        [q[:, (g * hpg + j) * HEAD_DIM:(g * hpg + j + 1) * HEAD_DIM] for j in range(hpg)], axis=0)


def _head_columns(x, first, hpg):
    return jnp.concatenate([x[:, first + j:first + j + 1] for j in range(hpg)], axis=0)


def _band_kernel(sink_ref, q_ref, kv_ref, o_ref, *, window, n_kv, tk):
    qb = q_ref.shape[1]
    hpg = N_HEADS // n_kv
    q0 = pl.program_id(1) * qb
    q = q_ref[0]
    r = lax.broadcasted_iota(jnp.int32, (hpg * qb, 1), 0)
    qpos = q0 + r % qb
    lane = lax.broadcasted_iota(jnp.int32, (1, tk), 1)
    lo = jnp.maximum(q0 - window, 0) // tk
    hi = (q0 + qb - 1) // tk + 1
    v_off = n_kv * HEAD_DIM
    for g in range(n_kv):
        qg = _stack_heads(q, g, hpg)

        def body(t, carry, g=g, qg=qg):
            rows = pl.ds(pl.multiple_of(t * tk, tk), tk)
            k = kv_ref[0, rows, g * HEAD_DIM:(g + 1) * HEAD_DIM]
            v = kv_ref[0, rows, v_off + g * HEAD_DIM:v_off + (g + 1) * HEAD_DIM]
            d = qpos - (t * tk + lane)
            return _osm_step(carry, _dot_nt(qg, k), (d >= 0) & (d <= window), v)

        m, l, acc = lax.fori_loop(lo, hi, body, _osm_init(hpg * qb))
        sink = jnp.zeros((hpg * qb, 1), F32)
        for j in range(hpg):
            sink = jnp.where(r // qb == j, sink_ref[g * hpg + j], sink)
        m_f = jnp.maximum(m, sink)
        scale = jnp.exp(m - m_f)
        out = acc * scale / (l * scale + jnp.exp(sink - m_f))
        for j in range(hpg):
            h = g * hpg + j
            o_ref[0, :, h * HEAD_DIM:(h + 1) * HEAD_DIM] = out[j * qb:(j + 1) * qb].astype(o_ref.dtype)


def _band_attention(q, kv_bf, sink, batch, seq_len, window, n_kv):
    qd = q.shape[1]
    qb = min(128, seq_len)
    tk = min(128, seq_len)
    cw = kv_bf.shape[1]
    out = pl.pallas_call(
        functools.partial(_band_kernel, window=window, n_kv=n_kv, tk=tk),
        grid=(batch, seq_len // qb),
        in_specs=[pl.BlockSpec(memory_space=pltpu.SMEM),
                  pl.BlockSpec((1, qb, qd), lambda b, i: (b, i, 0)),
                  pl.BlockSpec((1, seq_len, cw), lambda b, i: (b, 0, 0))],
        out_specs=pl.BlockSpec((1, qb, qd), lambda b, i: (b, i, 0)),
        out_shape=jax.ShapeDtypeStruct((batch, seq_len, qd), BF16),
        compiler_params=_cparams(("parallel", "parallel")), name="band_attention")(
            sink, q.reshape(batch, seq_len, qd), kv_bf.reshape(batch, seq_len, cw))
    return out.reshape(batch * seq_len, qd)


def _kmajor_init(lanes):
    return jnp.full((1, lanes), NEG, F32), jnp.zeros((VT_ROWS, lanes), F32)


def _kmajor_step(carry, s, vt, mask=None):
    m, acc = carry
    if mask is not None:
        s = jnp.where(mask, s, NEG)
    m_new = jnp.maximum(m, jnp.max(s, axis=0, keepdims=True))
    p = jnp.exp2(s - m_new).astype(BF16)
    acc = jnp.exp2(m - m_new) * acc + _dot(vt, p)
    return m_new, acc


def _kmajor_finish(carry):
    acc = carry[1]
    return acc[0:HEAD_DIM] / jnp.maximum(acc[HEAD_DIM:HEAD_DIM + 1], 1e-30)


def _value_selector(g, n_kv):
    r = lax.broadcasted_iota(jnp.int32, (VT_ROWS, n_kv * HEAD_DIM), 0)
    c = lax.broadcasted_iota(jnp.int32, (VT_ROWS, n_kv * HEAD_DIM), 1)
    sel = jnp.where((r < HEAD_DIM) & (c == g * HEAD_DIM + r), 1.0, 0.0).astype(BF16)
    ones_row = jnp.where(lax.broadcasted_iota(jnp.int32, (VT_ROWS, 1), 0) == HEAD_DIM, 1.0, 0.0)
    return sel, ones_row


def _stage_values(v_all, g, n_kv):
    sel, ones_row = _value_selector(g, n_kv)
    return (_dot_nt(sel, v_all) + ones_row).astype(BF16)


def _heads_t(q, c):
    return _dot_nt(_eye(LANES, BF16), q[:, c * LANES:(c + 1) * LANES])


def _store_heads_t(o_ref, out_t, g, hpg, qb):
    eye = _eye(qb, BF16)
    ob = out_t.astype(BF16)
    for c in range(hpg // 2):
        pair = jnp.concatenate([ob[:, (2 * c) * qb:(2 * c + 1) * qb],
                                ob[:, (2 * c + 1) * qb:(2 * c + 2) * qb]], axis=0)
        col = (g * hpg + 2 * c) * HEAD_DIM
        o_ref[0, :, col:col + LANES] = _dot_nt(eye, pair).astype(o_ref.dtype)


def _fox_consts():
    pk = np.zeros((FOX_KV, FOX_KV * HEAD_DIM, LANES), np.float32)
    pe = np.zeros((FOX_KV, 3, N_HEADS, LANES), np.float32)
    sq = np.zeros((FOX_KV, FOX_HPG, 3, LANES, N_HEADS), np.float32)
    cq = np.zeros((FOX_HPG, LANES, 1), np.float32)
    ck = np.zeros((1, LANES), np.float32)
    for g in range(FOX_KV):
        for d in range(HEAD_DIM):
            pk[g, g * HEAD_DIM + d, d] = 1.0
        for p in range(3):
            for j in range(FOX_HPG):
                pe[g, p, g * FOX_HPG + j, HEAD_DIM + 4 * p + j] = -1.0
                sq[g, j, p, HEAD_DIM + 12 + p, g * FOX_HPG + j] = 1.0
    for j in range(FOX_HPG):
        for p in range(3):
            cq[j, HEAD_DIM + 4 * p + j, 0] = 1.0
    ck[0, HEAD_DIM + 12:HEAD_DIM + 15] = 1.0
    as_bf = lambda a: jnp.asarray(a, BF16)
    return as_bf(pk), as_bf(pe), as_bf(sq), jnp.asarray(cq), jnp.asarray(ck)


def _fox_kernel(q_ref, kv_ref, lf_ref, pk_ref, pe_ref, sq_ref, cq_ref, ck_ref, o_ref,
                kp_ref, vt_ref, carry_ref):
    qb = q_ref.shape[1]
    i = pl.program_id(1)
    lanes = FOX_HPG * qb

    @pl.when(i == 0)
    def _():
        carry_ref[...] = jnp.zeros_like(carry_ref)

    d_tok = carry_ref[...] + _dot(_tri(qb, False), lf_ref[0], precision=HIGHEST)
    carry_ref[...] = d_tok[qb - 1:qb, :]
    e_parts = _split3_bf16(d_tok * LOG2E)
    q = q_ref[0]
    kv = kv_ref[0]
    w = FOX_KV * HEAD_DIM
    rows_i = pl.ds(pl.multiple_of(i * qb, qb), qb)
    krow = lax.broadcasted_iota(jnp.int32, (qb, 1), 0)
    qlane = lax.broadcasted_iota(jnp.int32, (1, lanes), 1) % qb
    for g in range(FOX_KV):
        kp = _dot(kv[:, 0:w], pk_ref[g]) + ck_ref[...]
        for p in range(3):
            kp = kp + _dot(e_parts[p], pe_ref[g, p])
        kp_ref[rows_i, g * LANES:(g + 1) * LANES] = kp.astype(BF16)
        vt_ref[i, g] = _stage_values(kv[:, w:2 * w], g, FOX_KV)

        pieces = []
        pairs_t = [_heads_t(q, g * FOX_HPG // 2 + c) for c in range(FOX_HPG // 2)]
        for j in range(FOX_HPG):
            qt = pairs_t[j // 2][(j % 2) * HEAD_DIM:(j % 2 + 1) * HEAD_DIM]
            aug = cq_ref[j] + jnp.zeros((LANES, qb), F32)
            for p in range(3):
                aug = aug + _dot_nt(sq_ref[g, j, p], e_parts[p])
            pieces.append(jnp.concatenate([qt, jnp.zeros((LANES - HEAD_DIM, qb), F32)], axis=0) + aug)
        q_aug = jnp.concatenate(pieces, axis=1).astype(BF16)

        def body(t, carry, g=g, q_aug=q_aug):
            rows = pl.ds(pl.multiple_of(t * qb, qb), qb)
            return _kmajor_step(carry, _dot(kp_ref[rows, g * LANES:(g + 1) * LANES], q_aug), vt_ref[t, g])

        carry = lax.fori_loop(0, i, body, _kmajor_init(lanes))
        s = _dot(kp_ref[rows_i, g * LANES:(g + 1) * LANES], q_aug)
        carry = _kmajor_step(carry, s, vt_ref[i, g], mask=krow <= qlane)
        _store_heads_t(o_ref, _kmajor_finish(carry), g, FOX_HPG, qb)


def _fox_attention(q, kv_bf, lf, batch, seq_len):
    qd = q.shape[1]
    qb = min(256, seq_len)
    cw = kv_bf.shape[1]
    consts = _fox_consts()
    out = pl.pallas_call(
        _fox_kernel, grid=(batch, seq_len // qb),
        in_specs=[pl.BlockSpec((1, qb, qd), lambda b, i: (b, i, 0)),
                  pl.BlockSpec((1, qb, cw), lambda b, i: (b, i, 0)),
                  pl.BlockSpec((1, qb, N_HEADS), lambda b, i: (b, i, 0))]
        + [_const_spec(a) for a in consts],
        out_specs=pl.BlockSpec((1, qb, qd), lambda b, i: (b, i, 0)),
        out_shape=jax.ShapeDtypeStruct((batch, seq_len, qd), BF16),
        scratch_shapes=[pltpu.VMEM((seq_len, FOX_KV * LANES), BF16),
                        pltpu.VMEM((seq_len // qb, FOX_KV, VT_ROWS, qb), BF16),
                        pltpu.VMEM((1, N_HEADS), F32)],
        compiler_params=_cparams(("parallel", "arbitrary")), name="fox_attention")(
            q.reshape(batch, seq_len, qd), kv_bf.reshape(batch, seq_len, cw),
            lf.reshape(batch, seq_len, N_HEADS), *consts)
    return out.reshape(batch * seq_len, qd)


def _compress_kernel(x_ref, w_ref, o_ref):
    rows, c = x_ref.shape
    x = x_ref[...].reshape(rows // NSA_BLK, NSA_BLK, c)
    o_ref[...] = jnp.sum(x * w_ref[...][None], axis=1)


def _compress(kv_f32, w_cmp):
    n = kv_f32.shape[0]
    tm = min(512, n)
    c = 2 * NSA_KV * HEAD_DIM
    return pl.pallas_call(
        _compress_kernel, grid=(n // tm,),
        in_specs=[pl.BlockSpec((tm, c), lambda i: (i, 0)), _resident((NSA_BLK, c), lambda i: (0, 0))],
        out_specs=pl.BlockSpec((tm // NSA_BLK, c), lambda i: (i, 0)),
        out_shape=jax.ShapeDtypeStruct((n // NSA_BLK, c), F32),
        compiler_params=_cparams(("parallel",)), name="nsa_compress")(kv_f32, w_cmp.reshape(NSA_BLK, c))


def _select_blocks(imp, cur, n_free, axis):
    nblk = imp.shape[axis]
    blk = lax.broadcasted_iota(jnp.int32, imp.shape, axis)
    forced = (blk == 0) | (blk == cur) | (blk == cur - 1)
    cand = jnp.where((blk <= cur) & jnp.logical_not(forced), imp, -1.0)
    sel = jnp.where(forced, 1.0, 0.0)
    blk_f = blk.astype(F32)
    for _ in range(n_free):
        m = jnp.max(cand, axis=axis, keepdims=True)
        first = jnp.min(jnp.where(cand == m, blk_f, float(nblk)), axis=axis, keepdims=True)
        hit = (blk_f == first) & (m >= 0.0)
        sel = jnp.where(hit, 1.0, sel)
        cand = jnp.where(hit, -1.0, cand)
    return sel


def _softmax(s, mask, axis, exp):
    s = jnp.where(mask, s, NEG)
    m = jnp.max(s, axis=axis, keepdims=True)
    e = jnp.where(mask, exp(s - m), 0.0)
    return e / jnp.maximum(jnp.sum(e, axis=axis, keepdims=True), 1e-30)


def _nsa_kernel(q_ref, gatet_ref, kvc_ref, kv_ref, onehot_ref, o_ref, kps_ref, kpw_ref, vts_ref, vtw_ref):
    qb = q_ref.shape[1]
    hpg = NSA_HPG
    lanes = hpg * qb
    nblk = kvc_ref.shape[1]
    i = pl.program_id(1)
    q0 = i * qb
    q = q_ref[0]
    kv = kv_ref[0]
    gates_t = gatet_ref[0]
    kvw = NSA_KV * HEAD_DIM
    rows_i = pl.ds(pl.multiple_of(q0, qb), qb)
    kps_ref[rows_i, 0:kvw] = kv[:, 0:kvw]
    kps_ref[rows_i, kvw:kvw + nblk] = onehot_ref[...]
    kpw_ref[rows_i, :] = kv[:, 2 * kvw:3 * kvw]
    for g in range(NSA_KV):
        vts_ref[i, g] = _stage_values(kv[:, kvw:2 * kvw], g, NSA_KV)
        vtw_ref[i, g] = _stage_values(kv[:, 3 * kvw:4 * kvw], g, NSA_KV)

    qpos = q0 + lax.broadcasted_iota(jnp.int32, (1, lanes), 1) % qb
    qpos1 = qpos[:, :qb]
    krow = lax.broadcasted_iota(jnp.int32, (qb, 1), 0)
    blk = lax.broadcasted_iota(jnp.int32, (nblk, 1), 0)
    lo_win = jnp.maximum(q0 - NSA_WIN, 0) // qb
    kc_hi, kc_lo = _split_bf16(kvc_ref[0, :, 0:kvw])
    vc = kvc_ref[0, :, kvw:2 * kvw].astype(BF16)
    for g in range(NSA_KV):
        pieces = []
        pairs_t = [_heads_t(q, g * hpg // 2 + c) for c in range(hpg // 2)]
        for j in range(hpg):
            qt = pairs_t[j // 2][(j % 2) * HEAD_DIM:(j % 2 + 1) * HEAD_DIM]
            zero = jnp.zeros((HEAD_DIM, qb), F32)
            pieces.append(jnp.concatenate([qt, zero] if g == 0 else [zero, qt], axis=0))
        q_top = jnp.concatenate(pieces, axis=1).astype(BF16)

        s_c = _dot(kc_hi, q_top) + _dot(kc_lo, q_top)
        p_c = _softmax(s_c, (blk + 1) * NSA_BLK - 1 <= qpos, 0, jnp.exp2)
        vct, _ = _value_selector(g, NSA_KV)
        o_cmp = _dot(_dot_nt(vct, vc).astype(BF16), p_c.astype(BF16))[0:HEAD_DIM]
        imp = p_c[:, 0:qb]
        for j in range(1, hpg):
            imp = imp + p_c[:, j * qb:(j + 1) * qb]
        sel = _select_blocks(imp, qpos1 // NSA_BLK, NSA_TOPK - NSA_N_FORCED, 0)
        penalty = ((sel - 1.0) * (-UNSELECTED)).astype(BF16)
        q_slc = jnp.concatenate([q_top, jnp.concatenate([penalty] * hpg, axis=1)], axis=0)

        def slc_body(t, carry, g=g, q_slc=q_slc):
            rows = pl.ds(pl.multiple_of(t * qb, qb), qb)
            return _kmajor_step(carry, _dot(kps_ref[rows, :], q_slc), vts_ref[t, g])

        carry = lax.fori_loop(0, i, slc_body, _kmajor_init(lanes))
        carry = _kmajor_step(carry, _dot(kps_ref[rows_i, :], q_slc), vts_ref[i, g], mask=q0 + krow <= qpos)
        o_slc = _kmajor_finish(carry)

        def win_body(t, carry, g=g, q_top=q_top):
            rows = pl.ds(pl.multiple_of(t * qb, qb), qb)
            d = qpos - (t * qb + krow)
            return _kmajor_step(carry, _dot(kpw_ref[rows, :], q_top), vtw_ref[t, g],
                                mask=(d >= 0) & (d <= NSA_WIN))

        o_win = _kmajor_finish(lax.fori_loop(lo_win, i + 1, win_body, _kmajor_init(lanes)))

        def gate_row(branch):
            first = branch * N_HEADS + g * hpg
            return jnp.concatenate([jnp.broadcast_to(gates_t[first + j:first + j + 1, :], (1, qb))
                                    for j in range(hpg)], axis=1)

        out_t = o_cmp * gate_row(0) + o_slc * gate_row(1) + o_win * gate_row(2)
        _store_heads_t(o_ref, out_t, g, hpg, qb)


def _block_onehot(n_keys, nblk, transposed=False):
    oh = jnp.arange(n_keys)[:, None] // NSA_BLK == jnp.arange(nblk)[None, :]
    return (oh.T if transposed else oh).astype(BF16)


def _nsa_attention(q, gates_t, kvc, kv_bf, batch, seq_len):
    qd = q.shape[1]
    qb = min(128, seq_len)
    nblk = seq_len // NSA_BLK
    cw = kv_bf.shape[1]
    kvw = NSA_KV * HEAD_DIM
    n_tiles = seq_len // qb
    out = pl.pallas_call(
        _nsa_kernel, grid=(batch, n_tiles),
        in_specs=[pl.BlockSpec((1, qb, qd), lambda b, i: (b, i, 0)),
                  pl.BlockSpec((1, LANES, qb), lambda b, i: (b, 0, i)),
                  pl.BlockSpec((1, nblk, kvc.shape[1]), lambda b, i: (b, 0, 0)),
                  pl.BlockSpec((1, qb, cw), lambda b, i: (b, i, 0)),
                  pl.BlockSpec((qb, nblk), lambda b, i: (i, 0))],
        out_specs=pl.BlockSpec((1, qb, qd), lambda b, i: (b, i, 0)),
        out_shape=jax.ShapeDtypeStruct((batch, seq_len, qd), BF16),
        scratch_shapes=[pltpu.VMEM((seq_len, kvw + nblk), BF16),
                        pltpu.VMEM((seq_len, kvw), BF16),
                        pltpu.VMEM((n_tiles, NSA_KV, VT_ROWS, qb), BF16),
                        pltpu.VMEM((n_tiles, NSA_KV, VT_ROWS, qb), BF16)],
        compiler_params=_cparams(("parallel", "arbitrary")), name="nsa_attention")(
            q.reshape(batch, seq_len, qd), gates_t, kvc.reshape(batch, nblk, kvc.shape[1]),
            kv_bf.reshape(batch, seq_len, cw), _block_onehot(seq_len, nblk))
    return out.reshape(batch * seq_len, qd)


N_ROWS = N_HEADS * SUBLANES


def _block_diag_queries(q, n_kv, qbd_ref):
    hpg = N_HEADS // n_kv
    qf = q.astype(F32)
    qbd_ref[...] = jnp.zeros_like(qbd_ref)
    for h in range(N_HEADS):
        g = h // hpg
        qbd_ref[h * SUBLANES:(h + 1) * SUBLANES, g * HEAD_DIM:(g + 1) * HEAD_DIM] = \
            qf[:, h * HEAD_DIM:(h + 1) * HEAD_DIM]
    return qbd_ref[...].astype(BF16)


def _pick_group(res, n_kv):
    hpg = N_HEADS // n_kv
    rg = lax.broadcasted_iota(jnp.int32, (N_ROWS, 1), 0) // (hpg * SUBLANES)
    out = res[:, 0:HEAD_DIM]
    for g in range(1, n_kv):
        out = jnp.where(rg == g, res[:, g * HEAD_DIM:(g + 1) * HEAD_DIM], out)
    return out


def _osm_step_bd(carry, s, mask, v_all, n_kv, v_transposed):
    m, l, acc = carry
    if mask is not None:
        s = jnp.where(mask, s, NEG)
    m_new = jnp.maximum(m, jnp.max(s, axis=-1, keepdims=True))
    p = jnp.exp(s - m_new)
    if mask is not None:
        p = jnp.where(mask, p, 0.0)
    alpha = jnp.exp(m - m_new)
    l = alpha * l + jnp.sum(p, axis=-1, keepdims=True)
    pv = _dot_nt(p.astype(BF16), v_all) if v_transposed else _dot(p.astype(BF16), v_all)
    acc = alpha * acc + _pick_group(pv, n_kv)
    return m_new, l, acc


def _pad_rows(x, rows):
    x = x.astype(F32)
    return jnp.concatenate([x, jnp.zeros((rows - x.shape[0], x.shape[1]), F32)], axis=0)


def _token_of_row():
    return lax.broadcasted_iota(jnp.int32, (N_ROWS, 1), 0) % SUBLANES


def _store_rows(o_ref, out, base):
    for h in range(N_HEADS):
        o_ref[base:base + SUBLANES, h * HEAD_DIM:(h + 1) * HEAD_DIM] = \
            out[h * SUBLANES:(h + 1) * SUBLANES].astype(o_ref.dtype)


def _new_token_step(carry, qbd, kn, vn, n_kv, bias=None):
    k_all = _pad_rows(kn, LANES).astype(BF16)
    v_all = _pad_rows(vn, LANES).astype(BF16)
    s = _dot_nt(qbd, k_all)
    if bias is not None:
        s = s + bias
    lane = lax.broadcasted_iota(jnp.int32, (1, LANES), 1)
    return _osm_step_bd(carry, s, lane <= _token_of_row(), v_all, n_kv, False)


def _window_step(carry, qbd, buf_ref, n_kv, window):
    wb = buf_ref.shape[-1]
    w = n_kv * HEAD_DIM
    lane = lax.broadcasted_iota(jnp.int32, (1, wb), 1)
    kt = buf_ref[0].reshape(w, wb).astype(BF16)
    vt = buf_ref[1].reshape(w, wb).astype(BF16)
    return _osm_step_bd(carry, _dot(qbd, kt), lane >= _token_of_row() + (wb - window), vt, n_kv, True)


def _swa_sample_kernel(sink_ref, q_ref, buf_ref, kvn_ref, o_ref, qbd_ref, *, window):
    n_seq = buf_ref.shape[0]
    w = SWA_KV * HEAD_DIM
    hrow = lax.broadcasted_iota(jnp.int32, (N_ROWS, 1), 0) // SUBLANES
    sink = jnp.zeros((N_ROWS, 1), F32)
    for h in range(N_HEADS):
        sink = jnp.where(hrow == h, sink_ref[h], sink)
    for s_i in range(n_seq):
        rows = slice(s_i * SUBLANES, (s_i + 1) * SUBLANES)
        qbd = _block_diag_queries(q_ref[rows, :], SWA_KV, qbd_ref)
        carry = _window_step(_osm_init(N_ROWS), qbd, buf_ref.at[s_i], SWA_KV, window)
        kvn = kvn_ref[rows, :]
        m, l, acc = _new_token_step(carry, qbd, kvn[:, 0:w], kvn[:, w:2 * w], SWA_KV)
        m_f = jnp.maximum(m, sink)
        scale = jnp.exp(m - m_f)
        out = acc * scale / (l * scale + jnp.exp(sink - m_f))
        _store_rows(o_ref, out, s_i * SUBLANES)


def _swa_sample_attention(q, state_t, layer, kvn_bf, sink):
    n_seq, wb = state_t.shape[1], state_t.shape[-1]
    sb = min(8, n_seq)
    qd = q.shape[1]
    cw = kvn_bf.shape[1]
    assert wb >= SWA_WIN
    return pl.pallas_call(
        functools.partial(_swa_sample_kernel, window=SWA_WIN), grid=(n_seq // sb,),
        in_specs=[pl.BlockSpec(memory_space=pltpu.SMEM),
                  pl.BlockSpec((sb * SUBLANES, qd), lambda i: (i, 0)),
                  pl.BlockSpec((None, sb, 2, SWA_KV, HEAD_DIM, wb), lambda i: (layer, i, 0, 0, 0, 0)),
                  pl.BlockSpec((sb * SUBLANES, cw), lambda i: (i, 0))],
        out_specs=pl.BlockSpec((sb * SUBLANES, qd), lambda i: (i, 0)),
        out_shape=jax.ShapeDtypeStruct(q.shape, BF16),
        scratch_shapes=[pltpu.VMEM((N_ROWS, SWA_KV * HEAD_DIM), F32)],
        compiler_params=_cparams(("parallel",)), name="swa_sample")(sink, q, state_t, kvn_bf)


def _page_specs(n, block, layer, slot_block, n_pages, per_step, last_chunk=None):
    trailing = (0,) * (len(block) - 1)

    def spec(k):
        def index_map(s, c, pt):
            cc = c if last_chunk is None else jnp.minimum(c, last_chunk)
            return (layer, pt[s * n_pages + cc * per_step + k], slot_block) + trailing
        return pl.BlockSpec((None, None) + block, index_map)

    return [spec(k) for k in range(n)]


def _fox_bias_kernel(pt_ref, *refs, ppc):
    page_refs = refs[:ppc]
    lfn_ref, dt_ref, dn_ref, carry_ref = refs[ppc:]
    c = pl.program_id(1)
    page = page_refs[0].shape[1]

    @pl.when(c == 0)
    def _():
        carry_ref[...] = jnp.zeros_like(carry_ref)

    stacked = jnp.concatenate([page_refs[k][...] for k in range(ppc)], axis=0)
    local = _dot(stacked, _tri(page, True), precision=HIGHEST)
    carry = carry_ref[...]
    for k in range(ppc):
        d = carry + local[k * N_HEADS:(k + 1) * N_HEADS]
        dt_ref[0, :, k * page:(k + 1) * page] = d
        carry = d[:, page - 1:page]
    carry_ref[...] = carry

    @pl.when(c == pl.num_programs(1) - 1)
    def _():
        dn_ref[0] = carry + _dot(lfn_ref[0], _tri(LANES, True), precision=HIGHEST)


def _fox_bias(logf_t, layer, pt_flat, n_seq, n_pages, lf_new_t):
    page = logf_t.shape[3]
    ppc = min(16, n_pages)
    assert page == LANES and n_pages % ppc == 0
    lfn = jnp.pad(lf_new_t, ((0, 0), (0, 0), (0, LANES - lf_new_t.shape[2])))
    grid_spec = pltpu.PrefetchScalarGridSpec(
        num_scalar_prefetch=1, grid=(n_seq, n_pages // ppc),
        in_specs=_page_specs(ppc, (N_HEADS, page), layer, 0, n_pages, ppc)
        + [pl.BlockSpec((1, N_HEADS, LANES), lambda s, c, pt: (s, 0, 0))],
        out_specs=[pl.BlockSpec((1, N_HEADS, ppc * page), lambda s, c, pt: (s, 0, c)),
                   pl.BlockSpec((1, N_HEADS, LANES), lambda s, c, pt: (s, 0, 0))],
        scratch_shapes=[pltpu.VMEM((N_HEADS, 1), F32)])
    return pl.pallas_call(
        functools.partial(_fox_bias_kernel, ppc=ppc), grid_spec=grid_spec,
        out_shape=[jax.ShapeDtypeStruct((n_seq, N_HEADS, n_pages * page), F32),
                   jax.ShapeDtypeStruct((n_seq, N_HEADS, LANES), F32)],
        compiler_params=_cparams(("parallel", "arbitrary")), name="fox_bias")(
            pt_flat, *([logf_t] * ppc), lfn)


def _rows_from_heads(x):
    return jnp.broadcast_to(x[:, None, :], (N_HEADS, SUBLANES, x.shape[1])).reshape(N_ROWS, x.shape[1])


def _stage_pages(page_refs, kt_ref, vt_ref):
    rows, page = kt_ref.shape[0], page_refs[0].shape[-1]
    for k, ref in enumerate(page_refs):
        kt_ref[:, k * page:(k + 1) * page] = ref[0].reshape(rows, page).astype(BF16)
        vt_ref[:, k * page:(k + 1) * page] = ref[1].reshape(rows, page).astype(BF16)


def _fox_sample_kernel(pt_ref, *refs, ppc):
    page_refs = refs[:ppc]
    (q_ref, dt_ref, dq_ref, dn_ref, kvn_ref, o_ref,
     qbd_ref, m_ref, l_ref, acc_ref, kt_ref, vt_ref) = refs[ppc:]
    c = pl.program_id(1)
    n_chunks = pl.num_programs(1) - 1
    w = FOX_KV * HEAD_DIM

    @pl.when(c == 0)
    def _():
        _block_diag_queries(q_ref[...], FOX_KV, qbd_ref)
        m_ref[...], l_ref[...], acc_ref[...] = _osm_init(N_ROWS)

    qbd = qbd_ref[...].astype(BF16)
    dq = dq_ref[0]

    @pl.when(c < n_chunks)
    def _():
        _stage_pages(page_refs, kt_ref, vt_ref)
        s = _dot(qbd, kt_ref[...]) + (dq - _rows_from_heads(dt_ref[0]))
        carry = _osm_step_bd((m_ref[...], l_ref[...], acc_ref[...]), s, None, vt_ref[...], FOX_KV, True)
        m_ref[...], l_ref[...], acc_ref[...] = carry

    @pl.when(c == n_chunks)
    def _():
        kvn = kvn_ref[...]
        carry = _new_token_step((m_ref[...], l_ref[...], acc_ref[...]), qbd, kvn[:, 0:w], kvn[:, w:2 * w],
                                FOX_KV, bias=dq - _rows_from_heads(dn_ref[0]))
        _store_rows(o_ref, _osm_finish(carry), 0)


def _fox_sample_attention(q, cache_t, layer, pt_flat, n_seq, n_pages, d_past, d_new, kvn_bf):
    page = cache_t.shape[-1]
    w = FOX_KV * HEAD_DIM
    ppc = min(8, n_pages)
    n_chunks = n_pages // ppc
    qd = q.shape[1]
    dq = d_new[:, :, :SUBLANES].reshape(n_seq, N_ROWS, 1)
    last = n_chunks - 1
    grid_spec = pltpu.PrefetchScalarGridSpec(
        num_scalar_prefetch=1, grid=(n_seq, n_chunks + 1),
        in_specs=_page_specs(ppc, (2, FOX_KV, HEAD_DIM, page), layer, 0, n_pages, ppc, last_chunk=last)
        + [pl.BlockSpec((SUBLANES, qd), lambda s, c, pt: (s, 0)),
           pl.BlockSpec((1, N_HEADS, ppc * page), lambda s, c, pt: (s, 0, jnp.minimum(c, last))),
           pl.BlockSpec((1, N_ROWS, 1), lambda s, c, pt: (s, 0, 0)),
           pl.BlockSpec((1, N_HEADS, LANES), lambda s, c, pt: (s, 0, 0)),
           pl.BlockSpec((SUBLANES, 2 * w), lambda s, c, pt: (s, 0))],
        out_specs=pl.BlockSpec((SUBLANES, qd), lambda s, c, pt: (s, 0)),
        scratch_shapes=[pltpu.VMEM((N_ROWS, w), F32),
                        pltpu.VMEM((N_ROWS, 1), F32), pltpu.VMEM((N_ROWS, 1), F32),
                        pltpu.VMEM((N_ROWS, HEAD_DIM), F32),
                        pltpu.VMEM((w, ppc * page), BF16), pltpu.VMEM((w, ppc * page), BF16)])
    return pl.pallas_call(
        functools.partial(_fox_sample_kernel, ppc=ppc), grid_spec=grid_spec,
        out_shape=jax.ShapeDtypeStruct(q.shape, BF16),
        compiler_params=_cparams(("parallel", "arbitrary")), name="fox_sample")(
            pt_flat, *([cache_t] * ppc), q, d_past, dq, d_new, kvn_bf)


def _compress_paged_kernel(pt_ref, *refs, ppc):
    page_refs = refs[:ppc]
    w_ref, o_ref = refs[ppc:]
    c = pl.program_id(1)
    page = page_refs[0].shape[-1]
    per_page = page // NSA_BLK

    @pl.when(c == 0)
    def _():
        o_ref[...] = jnp.zeros_like(o_ref)

    acc = o_ref[0]
    lane = lax.broadcasted_iota(jnp.int32, acc.shape, acc.ndim - 1)
    for k in range(ppc):
        prod = page_refs[k][...] * w_ref[...]
        for half in range(per_page):
            col = jnp.sum(prod[..., half * NSA_BLK:(half + 1) * NSA_BLK], axis=-1, keepdims=True)
            acc = jnp.where(lane == (c * ppc + k) * per_page + half, col, acc)
    o_ref[0] = acc


def _compress_paged(cache_t, layer, pt_flat, n_seq, n_pages, w_cmp):
    page = cache_t.shape[-1]
    per_page = page // NSA_BLK
    ppc = min(8, n_pages)
    assert n_pages % ppc == 0
    w_t = jnp.tile(jnp.transpose(w_cmp, (1, 2, 3, 0)), (1, 1, 1, per_page))
    blk = (2, NSA_KV, HEAD_DIM, page)
    nblk = n_pages * per_page
    grid_spec = pltpu.PrefetchScalarGridSpec(
        num_scalar_prefetch=1, grid=(n_seq, n_pages // ppc),
        in_specs=_page_specs(ppc, blk, layer, 0, n_pages, ppc)
        + [pl.BlockSpec(blk, lambda s, cc, pt: (0, 0, 0, 0))],
        out_specs=pl.BlockSpec((1, 2, NSA_KV, HEAD_DIM, nblk), lambda s, cc, pt: (s, 0, 0, 0, 0)))
    return pl.pallas_call(
        functools.partial(_compress_paged_kernel, ppc=ppc), grid_spec=grid_spec,
        out_shape=jax.ShapeDtypeStruct((n_seq, 2, NSA_KV, HEAD_DIM, nblk), F32),
        compiler_params=_cparams(("parallel", "arbitrary")), name="nsa_compress_paged")(
            pt_flat, *([cache_t] * ppc), w_t)


def _nsa_sample_kernel(pt_ref, *refs, ppc, past):
    page_refs = refs[:ppc]
    (q_ref, gate_ref, kvc_ref, onehot_ref, win_ref, kvn_ref, o_ref,
     qbd_ref, sel_ref, ocmp_ref, m_ref, l_ref, acc_ref, kt_ref, vt_ref) = refs[ppc:]
    c = pl.program_id(1)
    n_chunks = pl.num_programs(1) - 1
    w = NSA_KV * HEAD_DIM
    nblk = kvc_ref.shape[-1]
    tok = _token_of_row()

    @pl.when(c == 0)
    def _():
        qbd0 = _block_diag_queries(q_ref[...], NSA_KV, qbd_ref)
        kc_hi, kc_lo = _split_bf16(kvc_ref[0, 0].reshape(w, nblk))
        s_c = _dot(qbd0, kc_hi) + _dot(qbd0, kc_lo)
        blk = lax.broadcasted_iota(jnp.int32, (1, nblk), 1)
        p_c = _softmax(s_c, (blk + 1) * NSA_BLK - 1 <= past + tok, -1, jnp.exp)
        vct = kvc_ref[0, 1].reshape(w, nblk).astype(BF16)
        ocmp_ref[...] = _pick_group(_dot_nt(p_c.astype(BF16), vct), NSA_KV)
        imp = jnp.sum(p_c.reshape(NSA_KV, NSA_HPG, SUBLANES, nblk), axis=1)
        imp = imp.reshape(NSA_KV * SUBLANES, nblk)
        cur = (past + lax.broadcasted_iota(jnp.int32, (NSA_KV * SUBLANES, 1), 0) % SUBLANES) // NSA_BLK
        sel = _select_blocks(imp, cur, NSA_TOPK - NSA_N_FORCED, 1)
        sel = jnp.broadcast_to(sel.reshape(NSA_KV, 1, SUBLANES, nblk), (NSA_KV, NSA_HPG, SUBLANES, nblk))
        sel_ref[...] = sel.reshape(N_ROWS, nblk)
        m_ref[...], l_ref[...], acc_ref[...] = _osm_init(N_ROWS)

    qbd = qbd_ref[...].astype(BF16)

    @pl.when(c < n_chunks)
    def _():
        _stage_pages(page_refs, kt_ref, vt_ref)
        s = _dot(qbd, kt_ref[...])
        picked = _dot(sel_ref[...].astype(BF16), onehot_ref[...]) > 0.5
        carry = _osm_step_bd((m_ref[...], l_ref[...], acc_ref[...]), s, picked, vt_ref[...], NSA_KV, True)
        m_ref[...], l_ref[...], acc_ref[...] = carry

    @pl.when(c == n_chunks)
    def _():
        kvn = kvn_ref[...]
        o_slc = _osm_finish(_new_token_step((m_ref[...], l_ref[...], acc_ref[...]), qbd,
                                            kvn[:, 0:w], kvn[:, w:2 * w], NSA_KV))
        carry = _window_step(_osm_init(N_ROWS), qbd, win_ref, NSA_KV, NSA_WIN)
        o_win = _osm_finish(_new_token_step(carry, qbd, kvn[:, 2 * w:3 * w], kvn[:, 3 * w:4 * w], NSA_KV))
        gates = gate_ref[...]
        out = (ocmp_ref[...] * _head_columns(gates, 0, N_HEADS)
               + o_slc * _head_columns(gates, N_HEADS, N_HEADS)
               + o_win * _head_columns(gates, 2 * N_HEADS, N_HEADS))
        _store_rows(o_ref, out, 0)


def _nsa_sample_attention(q, gates, kvc_t, cache_t, layer, pt_flat, n_pages, win_t, kvn_bf):
    n_seq, nblk = kvc_t.shape[0], kvc_t.shape[-1]
    page = cache_t.shape[-1]
    past = n_pages * page
    ppc = min(8, n_pages)
    n_chunks = n_pages // ppc
    last = n_chunks - 1
    qd = q.shape[1]
    wb = win_t.shape[-1]
    w = NSA_KV * HEAD_DIM
    assert past % NSA_BLK == 0 and wb >= NSA_WIN and nblk == past // NSA_BLK and nblk >= NSA_TOPK
    grid_spec = pltpu.PrefetchScalarGridSpec(
        num_scalar_prefetch=1, grid=(n_seq, n_chunks + 1),
        in_specs=_page_specs(ppc, (2, NSA_KV, HEAD_DIM, page), layer, 1, n_pages, ppc, last_chunk=last)
        + [pl.BlockSpec((SUBLANES, qd), lambda s, cc, pt: (s, 0)),
           pl.BlockSpec((SUBLANES, LANES), lambda s, cc, pt: (s, 0)),
           pl.BlockSpec((1, 2, NSA_KV, HEAD_DIM, nblk), lambda s, cc, pt: (s, 0, 0, 0, 0)),
           pl.BlockSpec((nblk, ppc * page), lambda s, cc, pt: (0, jnp.minimum(cc, last))),
           pl.BlockSpec((None, None, 2, NSA_KV, HEAD_DIM, wb), lambda s, cc, pt: (layer, s, 0, 0, 0, 0)),
           pl.BlockSpec((SUBLANES, 4 * w), lambda s, cc, pt: (s, 0))],
        out_specs=pl.BlockSpec((SUBLANES, qd), lambda s, cc, pt: (s, 0)),
        scratch_shapes=[pltpu.VMEM((N_ROWS, w), F32),
                        pltpu.VMEM((N_ROWS, nblk), F32),
                        pltpu.VMEM((N_ROWS, HEAD_DIM), F32),
                        pltpu.VMEM((N_ROWS, 1), F32), pltpu.VMEM((N_ROWS, 1), F32),
                        pltpu.VMEM((N_ROWS, HEAD_DIM), F32),
                        pltpu.VMEM((w, ppc * page), BF16), pltpu.VMEM((w, ppc * page), BF16)])
    return pl.pallas_call(
        functools.partial(_nsa_sample_kernel, ppc=ppc, past=past), grid_spec=grid_spec,
        out_shape=jax.ShapeDtypeStruct(q.shape, BF16),
        compiler_params=_cparams(("parallel", "arbitrary")), name="nsa_sample")(
            pt_flat, *([cache_t] * ppc), q, gates, kvc_t, _block_onehot(past, nblk, transposed=True),
            win_t, kvn_bf)


def _pad_cols(w, cols):
    return jnp.pad(w, ((0, 0), (0, cols - w.shape[1])))


def _token_minor(x):
    return jnp.moveaxis(x, 2, -1)


def kernel(x_prompt, x_sample, page_table, cache_nsa_kv, state_nsa_win, state_swa_kv, cache_fox_kv,
           cache_fox_logf, state_ffn_conv, norm_mix, norm_ffn, norm_final, nsa_wq, nsa_wkv, nsa_cmp_w,
           nsa_wg, nsa_bg, nsa_wo, swa_wq, swa_wkv, swa_sink, swa_wo, fox_wq, fox_wkv, fox_wf, fox_bf,
           fox_wo, ffn_w_in, ffn_conv_w, ffn_conv_b, ffn_w_down):
    batch, seq, d = x_prompt.shape
    n_seq, dec, _ = x_sample.shape
    n_pages = page_table.shape[1]
    page = cache_nsa_kv.shape[2]
    past = n_pages * page
    depth = norm_mix.shape[0]
    assert dec == SUBLANES
    xp = x_prompt.reshape(batch * seq, d)
    xs = x_sample.reshape(n_seq * dec, d)
    pos_p = jnp.arange(seq)
    pos_s = past + jnp.arange(dec)
    pt_flat = page_table.reshape(-1)
    nsa_cache_t = _token_minor(cache_nsa_kv)
    nsa_win_t = _token_minor(state_nsa_win)
    swa_state_t = _token_minor(state_swa_kv)
    fox_cache_t = _token_minor(cache_fox_kv)
    fox_logf_t = _token_minor(cache_fox_logf)
    base2 = ATTN_SCALE * LOG2E

    outs = {k: [] for k in ("nsa_kv_p", "nsa_kv_s", "nsa_win_p", "nsa_win_s", "swa_p", "swa_s",
                            "fox_kv_p", "fox_kv_s", "fox_lf_p", "fox_lf_s", "conv_p", "conv_s")}
    for i in range(depth):
        j, kind = i // N_MIXERS, i % N_MIXERS
        g_mix = norm_mix[i]
        if kind == 0:
            wg = _pad_cols(nsa_wg[j], LANES)
            w_cat = jnp.concatenate([nsa_wq[j], nsa_wkv[j], wg], axis=1).astype(BF16)
            extras = {"bg": _pad_cols(nsa_bg[j].reshape(1, -1), LANES), "wg_t": wg.T.astype(BF16)}
            q, kv, win, kvbf, _, gates_t = _project(xp, seq, pos_p, g_mix, w_cat, "nsa", extras, base2)
            kvc = _compress(kv, nsa_cmp_w[j])
            o_p = _nsa_attention(q, _untile_cols(gates_t, batch, seq), kvc, kvbf, batch, seq)
            outs["nsa_kv_p"].append(kv.reshape(batch, seq, 4, NSA_KV, HEAD_DIM))
            outs["nsa_win_p"].append(win.reshape(batch, seq, 2, NSA_KV, HEAD_DIM)[:, seq - min(NSA_WIN, seq):])

            q, kv, win, kvbf, gates, _ = _project(xs, dec, pos_s, g_mix, w_cat, "nsa", extras, ATTN_SCALE)
            kvc_t = _compress_paged(nsa_cache_t, j, pt_flat, n_seq, n_pages, nsa_cmp_w[j])
            o_s = _nsa_sample_attention(q, gates, kvc_t, nsa_cache_t, j, pt_flat, n_pages, nsa_win_t, kvbf)
            outs["nsa_kv_s"].append(kv.reshape(n_seq, dec, 4, NSA_KV, HEAD_DIM))
            win_buf = state_nsa_win[j]
            win_new = win.reshape(n_seq, dec, 2, NSA_KV, HEAD_DIM)
            outs["nsa_win_s"].append(jnp.concatenate([win_buf, win_new], axis=1)[:, -win_buf.shape[1]:])
            wo = nsa_wo[j]
        elif kind == 1:
            w_cat = jnp.concatenate([swa_wq[j], swa_wkv[j]], axis=1).astype(BF16)
            q, kv, kvbf = _project(xp, seq, pos_p, g_mix, w_cat, "swa", None, ATTN_SCALE)
            o_p = _band_attention(q, kvbf, swa_sink[j], batch, seq, SWA_WIN, SWA_KV)
            outs["swa_p"].append(kv.reshape(batch, seq, 2, SWA_KV, HEAD_DIM)[:, seq - min(SWA_WIN, seq):])

            q, kv, kvbf = _project(xs, dec, pos_s, g_mix, w_cat, "swa", None, ATTN_SCALE)
            o_s = _swa_sample_attention(q, swa_state_t, j, kvbf, swa_sink[j])
            buf = state_swa_kv[j]
            kv_new = kv.reshape(n_seq, dec, 2, SWA_KV, HEAD_DIM)
            outs["swa_s"].append(jnp.concatenate([buf, kv_new], axis=1)[:, -buf.shape[1]:])
            wo = swa_wo[j]
        else:
            w_cat = jnp.concatenate([fox_wq[j], fox_wkv[j]], axis=1).astype(BF16)
            extras = {"wf": fox_wf[j], "bf": fox_bf[j]}
            q, kv, kvbf, lf, _ = _project(xp, seq, pos_p, g_mix, w_cat, "fox", extras, base2)
            o_p = _fox_attention(q, kvbf, lf, batch, seq)
            outs["fox_kv_p"].append(kv.reshape(batch, seq, 2, FOX_KV, HEAD_DIM))
            outs["fox_lf_p"].append(lf.reshape(batch, seq, N_HEADS))

            q, kv, kvbf, lf, lft = _project(xs, dec, pos_s, g_mix, w_cat, "fox", extras, ATTN_SCALE)
            d_past, d_new = _fox_bias(fox_logf_t, j, pt_flat, n_seq, n_pages, _untile_cols(lft, n_seq, dec))
            o_s = _fox_sample_attention(q, fox_cache_t, j, pt_flat, n_seq, n_pages, d_past, d_new, kvbf)
            outs["fox_kv_s"].append(kv.reshape(n_seq, dec, 2, FOX_KV, HEAD_DIM))
            outs["fox_lf_s"].append(lf.reshape(n_seq, dec, N_HEADS))
            wo = fox_wo[j]

        wo_bf, w_in_bf, w_down_bf = wo.astype(BF16), ffn_w_in[i].astype(BF16), ffn_w_down[i].astype(BF16)
        xp, conv_p = _post(xp, o_p, seq, wo_bf, norm_ffn[i], w_in_bf, ffn_conv_w[i], ffn_conv_b[i], w_down_bf, None)
        xs, conv_s = _post(xs, o_s, dec, wo_bf, norm_ffn[i], w_in_bf, ffn_conv_w[i], ffn_conv_b[i], w_down_bf,
                           state_ffn_conv[i])
        outs["conv_p"].append(conv_p)
        outs["conv_s"].append(conv_s)

    y_prompt = _final_norm(xp, norm_final).reshape(batch, seq, d)
    y_sample = _final_norm(xs, norm_final).reshape(n_seq, dec, d)
    st = lambda k: jnp.stack(outs[k])
    return (y_prompt, y_sample, st("nsa_kv_p"), st("nsa_kv_s"), st("nsa_win_p"), st("nsa_win_s"),
            st("swa_p"), st("swa_s"), st("fox_kv_p"), st("fox_kv_s"), st("fox_lf_p"), st("fox_lf_s"),
            st("conv_p"), st("conv_s"))
```

```python
import functools
import math

import numpy as np
import jax
import jax.numpy as jnp
from jax import lax
from jax.experimental import pallas as pl
from jax.experimental.pallas import tpu as pltpu

F32 = jnp.float32
BF16 = jnp.bfloat16
HIGHEST = lax.Precision.HIGHEST

HEAD_DIM = 64
N_HEADS = 16
ATTN_SCALE = HEAD_DIM ** -0.5
LOG2E = math.log2(math.e)
ROPE_THETA = 10000.0
RMS_EPS = 1e-6
N_MIXERS = 3
NSA_KV = 2
NSA_HPG = N_HEADS // NSA_KV
NSA_BLK = 64
NSA_TOPK = 16
NSA_N_FORCED = 3
NSA_WIN = 512
SWA_KV = 2
SWA_WIN = 128
FOX_KV = 4
FOX_HPG = N_HEADS // FOX_KV
CONV_W = 3
LANES = 128
SUBLANES = 8
NEG = -1e30
UNSELECTED = -32768.0
VT_ROWS = 80
VMEM_LIMIT = 56 * 1024 * 1024
PAGES_PER_STEP = 16


def _cparams(sem):
    return pltpu.CompilerParams(dimension_semantics=sem, vmem_limit_bytes=VMEM_LIMIT)


def _resident(shape, index_map):
    return pl.BlockSpec(shape, index_map, pipeline_mode=pl.Buffered(1))


def _const_spec(a):
    nd = a.ndim
    return _resident(a.shape, lambda *_: (0,) * nd)


def _dot(a, b, precision=None):
    return jnp.dot(a, b, precision=precision, preferred_element_type=F32)


def _dot_nt(a, b, precision=None):
    return lax.dot_general(a, b, (((1,), (1,)), ((), ())), precision=precision,
                           preferred_element_type=F32)


def _rms(x, g):
    ms = jnp.mean(x * x, axis=-1, keepdims=True)
    return x * lax.rsqrt(ms + RMS_EPS) * g


def _rope_chunk(y, cos, sin_signed, first_half):
    swapped = jnp.where(first_half, pltpu.roll(y, LANES - 32, 1), pltpu.roll(y, 32, 1))
    return y * cos + swapped * sin_signed


def _log_sigmoid(z):
    return jnp.minimum(z, 0.0) - jnp.log(1.0 + jnp.exp(-jnp.abs(z)))


def _eye(n, dtype):
    r = lax.broadcasted_iota(jnp.int32, (n, n), 0)
    c = lax.broadcasted_iota(jnp.int32, (n, n), 1)
    return jnp.where(r == c, 1.0, 0.0).astype(dtype)


def _tri(n, upper):
    r = lax.broadcasted_iota(jnp.int32, (n, n), 0)
    c = lax.broadcasted_iota(jnp.int32, (n, n), 1)
    return jnp.where((r <= c) if upper else (r >= c), 1.0, 0.0).astype(F32)


def _split_bf16(x):
    hi = x.astype(BF16)
    return hi, (x - hi.astype(F32)).astype(BF16)


def _split3_bf16(x):
    hi = x.astype(BF16)
    r = x - hi.astype(F32)
    mid = r.astype(BF16)
    return hi, mid, (r - mid.astype(F32)).astype(BF16)


def _proj_kernel(*refs, mode, has_rope, q_scale):
    it = iter(refs)
    x_ref, g_ref, w_ref = next(it), next(it), next(it)
    if has_rope:
        cos_ref, sin_ref = next(it), next(it)
    if mode == "nsa":
        bg_ref, wgt_ref, bgt_ref = next(it), next(it), next(it)
        q_out, kv_out, win_out, kvbf_out, gate_out, gatet_out = (next(it) for _ in range(6))
    elif mode == "swa":
        q_out, kv_out, kvbf_out = (next(it) for _ in range(3))
    else:
        wf_ref, wft_ref, bf_ref, bft_ref = (next(it) for _ in range(4))
        q_out, kv_out, kvbf_out, lf_out, lft_out = (next(it) for _ in range(5))

    hn = _rms(x_ref[...], g_ref[...])
    hb = hn.astype(BF16)
    y = _dot(hb, w_ref[...])
    if has_rope:
        cos, sin = cos_ref[...], sin_ref[...]
        first_half = (lax.broadcasted_iota(jnp.int32, cos.shape, 1) % HEAD_DIM) < HEAD_DIM // 2

    def chunk(c, rope):
        yc = y[:, c * LANES:(c + 1) * LANES]
        return _rope_chunk(yc, cos, sin, first_half) if rope else yc

    n_q = N_HEADS * HEAD_DIM // LANES
    for c in range(n_q):
        q_out[:, c * LANES:(c + 1) * LANES] = (chunk(c, has_rope) * q_scale).astype(BF16)
    if mode == "nsa":
        for c in range(6):
            yc = chunk(n_q + c, c % 2 == 0)
            if c < 4:
                kv_out[:, c * LANES:(c + 1) * LANES] = yc
            else:
                win_out[:, (c - 4) * LANES:(c - 3) * LANES] = yc
            if c >= 2:
                kvbf_out[:, (c - 2) * LANES:(c - 1) * LANES] = yc.astype(BF16)
        gate_out[...] = jax.nn.sigmoid(chunk(n_q + 6, False) + bg_ref[...])
        gatet_out[0] = jax.nn.sigmoid(_dot_nt(wgt_ref[...], hb) + bgt_ref[...])
    elif mode == "swa":
        for c in range(2):
            yc = chunk(n_q + c, c == 0)
            kv_out[:, c * LANES:(c + 1) * LANES] = yc
            kvbf_out[:, c * LANES:(c + 1) * LANES] = yc.astype(BF16)
    else:
        for c in range(4):
            yc = chunk(n_q + c, False)
            kv_out[:, c * LANES:(c + 1) * LANES] = yc
            kvbf_out[:, c * LANES:(c + 1) * LANES] = yc.astype(BF16)
        lf_out[...] = _log_sigmoid(_dot(hn, wf_ref[...], precision=HIGHEST) + bf_ref[...])
        lft_out[0] = _log_sigmoid(_dot_nt(wft_ref[...], hn, precision=HIGHEST) + bft_ref[...])


def _rope_tables(pos):
    half = HEAD_DIM // 2
    inv_freq = jnp.exp(-math.log(ROPE_THETA) * jnp.arange(half, dtype=F32) / half)
    ang = pos.astype(F32)[:, None] * inv_freq[None, :]
    cos, sin = jnp.cos(ang), jnp.sin(ang)
    return jnp.tile(cos, (1, 4)), jnp.concatenate([-sin, sin, -sin, sin], axis=1)


def _project(x, seq_len, pos, g, w_cat, mode, extras, q_scale):
    n, d = x.shape
    tm = min(512, n)
    assert n % tm == 0 and (seq_len % tm == 0 or tm % seq_len == 0)
    has_rope = mode != "fox"
    c_tot = w_cat.shape[1]
    in_specs = [pl.BlockSpec((tm, d), lambda i: (i, 0)),
                _resident((1, d), lambda i: (0, 0)),
                _resident((d, c_tot), lambda i: (0, 0))]
    args = [x, g.reshape(1, d), w_cat]
    if has_rope:
        cos, sin = _rope_tables(pos)
        if seq_len < tm:
            cos, sin = jnp.tile(cos, (tm // seq_len, 1)), jnp.tile(sin, (tm // seq_len, 1))
        period = cos.shape[0] // tm
        in_specs += [pl.BlockSpec((tm, LANES), lambda i: (i % period, 0))] * 2
        args += [cos, sin]
    qd = N_HEADS * HEAD_DIM
    row = lambda w: pl.BlockSpec((tm, w), lambda i: (i, 0))
    col = lambda h: pl.BlockSpec((1, h, tm), lambda i: (i, 0, 0))
    col_shape = lambda h: jax.ShapeDtypeStruct((n // tm, h, tm), F32)
    if mode == "nsa":
        bg, wg_t = extras["bg"], extras["wg_t"]
        in_specs += [_const_spec(bg), _const_spec(wg_t), _resident((LANES, 1), lambda i: (0, 0))]
        args += [bg, wg_t, bg.reshape(LANES, 1)]
        widths = [(qd, BF16), (512, F32), (256, F32), (512, BF16), (LANES, F32)]
        out_specs = [row(w) for w, _ in widths] + [col(LANES)]
        out_shape = [jax.ShapeDtypeStruct((n, w), dt) for w, dt in widths] + [col_shape(LANES)]
    elif mode == "swa":
        widths = [(qd, BF16), (256, F32), (256, BF16)]
        out_specs = [row(w) for w, _ in widths]
        out_shape = [jax.ShapeDtypeStruct((n, w), dt) for w, dt in widths]
    else:
        wf, bf = extras["wf"], extras["bf"]
        in_specs += [_resident((d, N_HEADS), lambda i: (0, 0)),
                     _resident((N_HEADS, d), lambda i: (0, 0)),
                     _resident((1, N_HEADS), lambda i: (0, 0)),
                     _resident((N_HEADS, 1), lambda i: (0, 0))]
        args += [wf, wf.T, bf.reshape(1, N_HEADS), bf.reshape(N_HEADS, 1)]
        widths = [(qd, BF16), (512, F32), (512, BF16), (N_HEADS, F32)]
        out_specs = [row(w) for w, _ in widths] + [col(N_HEADS)]
        out_shape = [jax.ShapeDtypeStruct((n, w), dt) for w, dt in widths] + [col_shape(N_HEADS)]
    return pl.pallas_call(
        functools.partial(_proj_kernel, mode=mode, has_rope=has_rope, q_scale=q_scale),
        grid=(n // tm,), in_specs=in_specs, out_specs=out_specs, out_shape=out_shape,
        compiler_params=_cparams(("parallel",)), name=f"proj_{mode}")(*args)


def _untile_cols(xt, batch, seq_len):
    tiles, h, tm = xt.shape
    if seq_len >= tm:
        return xt.reshape(batch, seq_len // tm, h, tm).transpose(0, 2, 1, 3).reshape(batch, h, seq_len)
    return xt.reshape(tiles, h, tm // seq_len, seq_len).transpose(0, 2, 1, 3).reshape(batch, h, seq_len)


def _post_kernel(*refs, per_seq_tiles, grouped, n_chunks):
    it = iter(refs)
    x_ref, o_ref, wo_ref, g_ref, win_ref, cw_ref, cb_ref, wd_ref = (next(it) for _ in range(8))
    if grouped:
        s0_ref, s1_ref = next(it), next(it)
    xo_ref, gtail_ref = next(it), next(it)
    if not grouped:
        carry_ref = next(it)

    tm = x_ref.shape[0]
    d_ff = wd_ref.shape[0]
    fc = d_ff // n_chunks
    n_groups, cg = o_ref.shape[1], o_ref.shape[3]
    x1 = x_ref[...]
    for g in range(n_groups):
        x1 = x1 + _dot(o_ref[0, g], wo_ref[g * cg:(g + 1) * cg, :])
    hb = _rms(x1, g_ref[...]).astype(BF16)
    row = lax.broadcasted_iota(jnp.int32, (tm, 1), 0)
    if grouped:
        r_in = row % SUBLANES
    else:
        r_in = row
        first_tile = pl.program_id(0) % per_seq_tiles == 0
    acc = jnp.zeros_like(x1)
    for c in range(n_chunks):
        cols = slice(c * fc, (c + 1) * fc)
        gch = _dot(hb, win_ref[:, c * fc:(c + 1) * fc])
        uch = _dot(hb, win_ref[:, d_ff + c * fc:d_ff + (c + 1) * fc])
        if grouped:
            prev1 = s1_ref[:, cols]
            prev2a, prev2b = s0_ref[:, cols], s1_ref[:, cols]
        else:
            prev2a = jnp.where(first_tile, 0.0, carry_ref[0:1, cols])
            prev1 = prev2b = jnp.where(first_tile, 0.0, carry_ref[1:2, cols])
        gm1 = jnp.where(r_in == 0, prev1, pltpu.roll(gch, 1, 0))
        gm2 = jnp.where(r_in == 0, prev2a, jnp.where(r_in == 1, prev2b, pltpu.roll(gch, 2, 0)))
        gc = cb_ref[:, cols] + cw_ref[0:1, cols] * gm2 + cw_ref[1:2, cols] * gm1 + cw_ref[2:3, cols] * gch
        a = gc * jax.nn.sigmoid(gc) * uch
        acc = acc + _dot(a.astype(BF16), wd_ref[c * fc:(c + 1) * fc, :])
        if grouped:
            gtail_ref[:, cols] = gch
        else:
            gtail_ref[0, :, cols] = gch[tm - SUBLANES:, :]
            carry_ref[0:2, cols] = gch[tm - 2:, :]
    xo_ref[...] = x1 + acc


def _post(x, o, seq_len, wo, g, w_in, conv_w, conv_b, w_down, conv_state):
    n, d = x.shape
    d_ff = w_down.shape[0]
    grouped = conv_state is not None
    tm = min(256 if grouped else 512, n)
    n_seq = n // seq_len
    if o.ndim == 2:
        o = o.reshape(1, 1, n, o.shape[1])
    tiles_per_batch = o.shape[2] // tm
    in_specs = [pl.BlockSpec((tm, d), lambda i: (i, 0)),
                pl.BlockSpec((1, o.shape[1], tm, o.shape[3]),
                             lambda i: (i // tiles_per_batch, 0, i % tiles_per_batch, 0)),
                _resident(wo.shape, lambda i: (0, 0)),
                _resident((1, d), lambda i: (0, 0)),
                _resident(w_in.shape, lambda i: (0, 0)),
                _resident((CONV_W, d_ff), lambda i: (0, 0)),
                _resident((1, d_ff), lambda i: (0, 0)),
                _resident(w_down.shape, lambda i: (0, 0))]
    args = [x, o, wo, g.reshape(1, d), w_in, conv_w, conv_b.reshape(1, d_ff), w_down]
    scratch = []
    if grouped:
        assert seq_len == SUBLANES
        in_specs += [pl.BlockSpec((tm, d_ff), lambda i: (i, 0))] * 2
        args += [jnp.repeat(conv_state[:, 0], seq_len, axis=0), jnp.repeat(conv_state[:, 1], seq_len, axis=0)]
        tail_spec = pl.BlockSpec((tm, d_ff), lambda i: (i, 0))
        tail_shape = jax.ShapeDtypeStruct((n, d_ff), F32)
        per_seq_tiles = 1
    else:
        assert seq_len % tm == 0
        per_seq_tiles = seq_len // tm
        tail_spec = pl.BlockSpec((1, SUBLANES, d_ff), lambda i: (i // per_seq_tiles, 0, 0))
        tail_shape = jax.ShapeDtypeStruct((n_seq, SUBLANES, d_ff), F32)
        scratch = [pltpu.VMEM((SUBLANES, d_ff), F32)]
    x_new, tail = pl.pallas_call(
        functools.partial(_post_kernel, per_seq_tiles=per_seq_tiles, grouped=grouped, n_chunks=2),
        grid=(n // tm,), in_specs=in_specs,
        out_specs=[pl.BlockSpec((tm, d), lambda i: (i, 0)), tail_spec],
        out_shape=[jax.ShapeDtypeStruct((n, d), F32), tail_shape],
        scratch_shapes=scratch,
        compiler_params=_cparams(("arbitrary",)), name="post_ffn")(*args)
    if grouped:
        conv_new = tail.reshape(n_seq, seq_len, d_ff)[:, seq_len - (CONV_W - 1):]
    else:
        conv_new = tail[:, SUBLANES - (CONV_W - 1):]
    return x_new, conv_new


def _final_norm_kernel(x_ref, g_ref, o_ref):
    o_ref[...] = _rms(x_ref[...], g_ref[...])


def _final_norm(x, g):
    n, d = x.shape
    tm = min(1024, n)
    return pl.pallas_call(
        _final_norm_kernel, grid=(n // tm,),
        in_specs=[pl.BlockSpec((tm, d), lambda i: (i, 0)), _resident((1, d), lambda i: (0, 0))],
        out_specs=pl.BlockSpec((tm, d), lambda i: (i, 0)),
        out_shape=jax.ShapeDtypeStruct((n, d), F32),
        compiler_params=_cparams(("parallel",)), name="final_norm")(x, g.reshape(1, d))


def _osm_init(rows):
    return (jnp.full((rows, 1), NEG, F32), jnp.zeros((rows, 1), F32), jnp.zeros((rows, HEAD_DIM), F32))


def _osm_step(carry, s, mask, v):
    m, l, acc = carry
    s = jnp.where(mask, s, NEG)
    m_new = jnp.maximum(m, jnp.max(s, axis=-1, keepdims=True))
    p = jnp.where(mask, jnp.exp(s - m_new), 0.0)
    alpha = jnp.exp(m - m_new)
    l = alpha * l + jnp.sum(p, axis=-1, keepdims=True)
    acc = alpha * acc + _dot(p.astype(BF16), v)
    return m_new, l, acc


def _osm_finish(carry):
    m, l, acc = carry
    return acc / jnp.maximum(l, 1e-30)


def _stack_heads(q, g, hpg):
    return jnp.concatenate(
        [q[:, (g * hpg + j) * HEAD_DIM:(g * hpg + j + 1) * HEAD_DIM] for j in range(hpg)], axis=0)


def _head_columns(x, first, hpg):
    return jnp.concatenate([x[:, first + j:first + j + 1] for j in range(hpg)], axis=0)


def _band_kernel(sink_ref, q_ref, kv_ref, o_ref, *, window, n_kv, tk):
    qb = q_ref.shape[1]
    hpg = N_HEADS // n_kv
    q0 = pl.program_id(1) * qb
    q = q_ref[0]
    r = lax.broadcasted_iota(jnp.int32, (hpg * qb, 1), 0)
    qpos = q0 + r % qb
    lane = lax.broadcasted_iota(jnp.int32, (1, tk), 1)
    lo = jnp.maximum(q0 - window, 0) // tk
    hi = (q0 + qb - 1) // tk + 1
    v_off = n_kv * HEAD_DIM
    for g in range(n_kv):
        qg = _stack_heads(q, g, hpg)

        def body(t, carry, g=g, qg=qg):
            rows = pl.ds(pl.multiple_of(t * tk, tk), tk)
            k = kv_ref[0, rows, g * HEAD_DIM:(g + 1) * HEAD_DIM]
            v = kv_ref[0, rows, v_off + g * HEAD_DIM:v_off + (g + 1) * HEAD_DIM]
            d = qpos - (t * tk + lane)
            return _osm_step(carry, _dot_nt(qg, k), (d >= 0) & (d <= window), v)

        m, l, acc = lax.fori_loop(lo, hi, body, _osm_init(hpg * qb))
        sink = jnp.zeros((hpg * qb, 1), F32)
        for j in range(hpg):
            sink = jnp.where(r // qb == j, sink_ref[g * hpg + j], sink)
        m_f = jnp.maximum(m, sink)
        scale = jnp.exp(m - m_f)
        out = acc * scale / (l * scale + jnp.exp(sink - m_f))
        for j in range(hpg):
            h = g * hpg + j
            o_ref[0, :, h * HEAD_DIM:(h + 1) * HEAD_DIM] = out[j * qb:(j + 1) * qb].astype(o_ref.dtype)


def _band_attention(q, kv_bf, sink, batch, seq_len, window, n_kv):
    qd = q.shape[1]
    qb = min(128, seq_len)
    tk = min(128, seq_len)
    cw = kv_bf.shape[1]
    out = pl.pallas_call(
        functools.partial(_band_kernel, window=window, n_kv=n_kv, tk=tk),
        grid=(batch, seq_len // qb),
        in_specs=[pl.BlockSpec(memory_space=pltpu.SMEM),
                  pl.BlockSpec((1, qb, qd), lambda b, i: (b, i, 0)),
                  pl.BlockSpec((1, seq_len, cw), lambda b, i: (b, 0, 0))],
        out_specs=pl.BlockSpec((1, qb, qd), lambda b, i: (b, i, 0)),
        out_shape=jax.ShapeDtypeStruct((batch, seq_len, qd), BF16),
        compiler_params=_cparams(("parallel", "parallel")), name="band_attention")(
            sink, q.reshape(batch, seq_len, qd), kv_bf.reshape(batch, seq_len, cw))
    return out.reshape(batch * seq_len, qd)


KMAJOR_CHUNK = 256


def _kmajor_scratch(tk, lanes):
    return [pltpu.VMEM((tk, lanes), F32), pltpu.VMEM((tk, lanes), F32),
            pltpu.VMEM((1, lanes), F32), pltpu.VMEM((VT_ROWS, lanes), F32)]


def _kmajor_attention(k_tile, qw, vt_at, lo, last, mask_fn, mask_all, bufs):
    s_a, s_b, m_ref, acc_ref = bufs
    lanes = qw.shape[1]
    cw = min(KMAJOR_CHUNK, lanes)
    m_ref[...] = jnp.full(m_ref.shape, NEG, F32)
    acc_ref[...] = jnp.zeros(acc_ref.shape, F32)
    s_a[...] = _dot(k_tile(lo), qw)

    def stage(t, s_cur, s_nxt, masked, is_last):
        vt = vt_at(t)
        k_next = None if is_last else k_tile(t + 1)
        mask = mask_fn(t) if masked else None
        zero = None
        for c in range(lanes // cw):
            cols = slice(c * cw, (c + 1) * cw)
            if not is_last:
                w_c = qw[:, cols] if zero is None else qw[:, cols] + zero
                s_nxt[:, cols] = _dot(k_next, w_c)
            s = s_cur[:, cols]
            if masked:
                s = jnp.where(mask[:, cols], s, NEG)
            m_old = m_ref[:, cols]
            m_new = jnp.maximum(m_old, jnp.max(s, axis=0, keepdims=True))
            e = jnp.exp2(s - m_new)
            acc_ref[:, cols] = jnp.exp2(m_old - m_new) * acc_ref[:, cols] + _dot(vt, e.astype(BF16))
            m_ref[:, cols] = m_new
            zero = jnp.minimum(e[e.shape[0] - 1:, :], 0.0).astype(BF16)

    def trip(t, s_cur, s_nxt):
        @pl.when(t < last)
        def _():
            stage(t, s_cur, s_nxt, mask_all, False)

        @pl.when(t == last)
        def _():
            stage(t, s_cur, s_nxt, True, True)

    def pair(u, carry):
        t = lo + 2 * u
        trip(t, s_a, s_b)
        trip(t + 1, s_b, s_a)
        return carry

    lax.fori_loop(0, (last - lo) // 2 + 1, pair, 0)
    acc = acc_ref[...]
    return acc[0:HEAD_DIM] / jnp.maximum(acc[HEAD_DIM:HEAD_DIM + 1], 1e-30)


def _value_selector(g, n_kv):
    r = lax.broadcasted_iota(jnp.int32, (VT_ROWS, n_kv * HEAD_DIM), 0)
    c = lax.broadcasted_iota(jnp.int32, (VT_ROWS, n_kv * HEAD_DIM), 1)
    sel = jnp.where((r < HEAD_DIM) & (c == g * HEAD_DIM + r), 1.0, 0.0).astype(BF16)
    ones_row = jnp.where(lax.broadcasted_iota(jnp.int32, (VT_ROWS, 1), 0) == HEAD_DIM, 1.0, 0.0)
    return sel, ones_row


def _stage_values(v_all, g, n_kv):
    sel, ones_row = _value_selector(g, n_kv)
    return (_dot_nt(sel, v_all) + ones_row).astype(BF16)


def _heads_t(q, c):
    return _dot_nt(_eye(LANES, BF16), q[:, c * LANES:(c + 1) * LANES])


def _store_heads_t(o_ref, out_t, hpg, qb):
    eye = _eye(qb, BF16)
    ob = out_t.astype(BF16)
    for c in range(hpg // 2):
        pair = jnp.concatenate([ob[:, (2 * c) * qb:(2 * c + 1) * qb],
                                ob[:, (2 * c + 1) * qb:(2 * c + 2) * qb]], axis=0)
        o_ref[0, 0, :, c * LANES:(c + 1) * LANES] = _dot_nt(eye, pair).astype(o_ref.dtype)


def _group_queries_t(q, hpg):
    pairs_t = [_heads_t(q, c) for c in range(hpg // 2)]
    return [pairs_t[j // 2][(j % 2) * HEAD_DIM:(j % 2 + 1) * HEAD_DIM] for j in range(hpg)]


def _fox_consts():
    pk = np.zeros((FOX_KV, FOX_KV * HEAD_DIM, LANES), np.float32)
    pe = np.zeros((FOX_KV, 3, N_HEADS, LANES), np.float32)
    sq = np.zeros((FOX_KV, FOX_HPG, 3, LANES, N_HEADS), np.float32)
    cq = np.zeros((FOX_HPG, LANES, 1), np.float32)
    ck = np.zeros((1, LANES), np.float32)
    for g in range(FOX_KV):
        for d in range(HEAD_DIM):
            pk[g, g * HEAD_DIM + d, d] = 1.0
        for p in range(3):
            for j in range(FOX_HPG):
                pe[g, p, g * FOX_HPG + j, HEAD_DIM + 4 * p + j] = -1.0
                sq[g, j, p, HEAD_DIM + 12 + p, g * FOX_HPG + j] = 1.0
    for j in range(FOX_HPG):
        for p in range(3):
            cq[j, HEAD_DIM + 4 * p + j, 0] = 1.0
    ck[0, HEAD_DIM + 12:HEAD_DIM + 15] = 1.0
    as_bf = lambda a: jnp.asarray(a, BF16)
    return as_bf(pk), as_bf(pe), as_bf(sq), jnp.asarray(cq), jnp.asarray(ck)


def _fox_kernel(q_ref, kv_ref, lf_ref, pk_ref, pe_ref, sq_ref, cq_ref, ck_ref, o_ref,
                kp_ref, vt_ref, carry_ref, e_ref, *bufs):
    qb = q_ref.shape[1]
    i = pl.program_id(1)
    g = pl.program_id(2)
    lanes = FOX_HPG * qb

    @pl.when(g == 0)
    def _():
        prev = jnp.where(i == 0, 0.0, carry_ref[...])
        d_tok = prev + _dot(_tri(qb, False), lf_ref[0], precision=HIGHEST)
        carry_ref[...] = d_tok[qb - 1:qb, :]
        e_ref[...] = d_tok * LOG2E

    e_parts = _split3_bf16(e_ref[...])
    kv = kv_ref[0]
    w = FOX_KV * HEAD_DIM
    rows_i = pl.ds(pl.multiple_of(i * qb, qb), qb)
    kp = _dot(kv[:, 0:w], pk_ref[g]) + ck_ref[...]
    for p in range(3):
        kp = kp + _dot(e_parts[p], pe_ref[g, p])
    kp_ref[g, rows_i, :] = kp.astype(BF16)
    vt_ref[i, g] = _stage_values(kv[:, w:2 * w], g, FOX_KV)

    pieces = []
    for j, qt in enumerate(_group_queries_t(q_ref[0], FOX_HPG)):
        aug = cq_ref[j] + jnp.zeros((LANES, qb), F32)
        for p in range(3):
            aug = aug + _dot_nt(sq_ref[g, j, p], e_parts[p])
        pieces.append(jnp.concatenate([qt, jnp.zeros((LANES - HEAD_DIM, qb), F32)], axis=0) + aug)
    q_aug = jnp.concatenate(pieces, axis=1).astype(BF16)

    krow = lax.broadcasted_iota(jnp.int32, (qb, 1), 0)
    qlane = lax.broadcasted_iota(jnp.int32, (1, lanes), 1) % qb

    def k_tile(t):
        return kp_ref[g, pl.ds(pl.multiple_of(t * qb, qb), qb), :]

    out_t = _kmajor_attention(k_tile, q_aug, lambda t: vt_ref[t, g], 0, i, lambda t: krow <= qlane, False, bufs)
    _store_heads_t(o_ref, out_t, FOX_HPG, qb)


def _fox_attention(q, kv_bf, lf, batch, seq_len):
    qb = min(256, seq_len)
    cw = kv_bf.shape[1]
    gw = FOX_HPG * HEAD_DIM
    n_tiles = seq_len // qb
    consts = _fox_consts()
    return pl.pallas_call(
        _fox_kernel, grid=(batch, n_tiles, FOX_KV),
        in_specs=[pl.BlockSpec((1, qb, gw), lambda b, i, g: (b, i, g)),
                  pl.BlockSpec((1, qb, cw), lambda b, i, g: (b, i, 0)),
                  pl.BlockSpec((1, qb, N_HEADS), lambda b, i, g: (b, i, 0))]
        + [_const_spec(a) for a in consts],
        out_specs=pl.BlockSpec((1, 1, qb, gw), lambda b, i, g: (b, g, i, 0)),
        out_shape=jax.ShapeDtypeStruct((batch, FOX_KV, seq_len, gw), BF16),
        scratch_shapes=[pltpu.VMEM((FOX_KV, seq_len, LANES), BF16),
                        pltpu.VMEM((n_tiles, FOX_KV, VT_ROWS, qb), BF16),
                        pltpu.VMEM((1, N_HEADS), F32), pltpu.VMEM((qb, N_HEADS), F32)]
        + _kmajor_scratch(qb, FOX_HPG * qb),
        compiler_params=_cparams(("parallel", "arbitrary", "arbitrary")), name="fox_attention")(
            q.reshape(batch, seq_len, q.shape[1]), kv_bf.reshape(batch, seq_len, cw),
            lf.reshape(batch, seq_len, N_HEADS), *consts)


def _compress_kernel(x_ref, w_ref, o_ref):
    rows, c = x_ref.shape
    x = x_ref[...].reshape(rows // NSA_BLK, NSA_BLK, c)
    o_ref[...] = jnp.sum(x * w_ref[...][None], axis=1)


def _compress(kv_f32, w_cmp):
    n = kv_f32.shape[0]
    tm = min(512, n)
    c = 2 * NSA_KV * HEAD_DIM
    return pl.pallas_call(
        _compress_kernel, grid=(n // tm,),
        in_specs=[pl.BlockSpec((tm, c), lambda i: (i, 0)), _resident((NSA_BLK, c), lambda i: (0, 0))],
        out_specs=pl.BlockSpec((tm // NSA_BLK, c), lambda i: (i, 0)),
        out_shape=jax.ShapeDtypeStruct((n // NSA_BLK, c), F32),
        compiler_params=_cparams(("parallel",)), name="nsa_compress")(kv_f32, w_cmp.reshape(NSA_BLK, c))


def _select_blocks(imp, cur, n_free, axis):
    nblk = imp.shape[axis]
    blk = lax.broadcasted_iota(jnp.int32, imp.shape, axis)
    forced = (blk == 0) | (blk == cur) | (blk == cur - 1)
    cand = jnp.where((blk <= cur) & jnp.logical_not(forced), imp, -1.0)
    sel = jnp.where(forced, 1.0, 0.0)
    blk_f = blk.astype(F32)
    for _ in range(n_free):
        m = jnp.max(cand, axis=axis, keepdims=True)
        first = jnp.min(jnp.where(cand == m, blk_f, float(nblk)), axis=axis, keepdims=True)
        hit = (blk_f == first) & (m >= 0.0)
        sel = jnp.where(hit, 1.0, sel)
        cand = jnp.where(hit, -1.0, cand)
    return sel


def _softmax(s, mask, axis, exp):
    s = jnp.where(mask, s, NEG)
    m = jnp.max(s, axis=axis, keepdims=True)
    e = jnp.where(mask, exp(s - m), 0.0)
    return e / jnp.maximum(jnp.sum(e, axis=axis, keepdims=True), 1e-30)


def _nsa_kernel(q_ref, gatet_ref, kvc_ref, kv_ref, onehot_ref, o_ref, kps_ref, kpw_ref, vts_ref, vtw_ref,
                *bufs):
    qb = q_ref.shape[1]
    hpg = NSA_HPG
    lanes = hpg * qb
    nblk = kvc_ref.shape[1]
    i = pl.program_id(1)
    g = pl.program_id(2)
    q0 = i * qb
    kv = kv_ref[0]
    kvw = NSA_KV * HEAD_DIM
    rows_i = pl.ds(pl.multiple_of(q0, qb), qb)

    @pl.when(g == 0)
    def _():
        kps_ref[rows_i, 0:kvw] = kv[:, 0:kvw]
        kps_ref[rows_i, kvw:kvw + nblk] = onehot_ref[...]
        kpw_ref[rows_i, :] = kv[:, 2 * kvw:3 * kvw]

    vts_ref[i, g] = _stage_values(kv[:, kvw:2 * kvw], g, NSA_KV)
    vtw_ref[i, g] = _stage_values(kv[:, 3 * kvw:4 * kvw], g, NSA_KV)

    qpos = q0 + lax.broadcasted_iota(jnp.int32, (1, lanes), 1) % qb
    qpos1 = qpos[:, :qb]
    krow = lax.broadcasted_iota(jnp.int32, (qb, 1), 0)
    blk = lax.broadcasted_iota(jnp.int32, (nblk, 1), 0)
    lo_win = jnp.maximum(q0 - NSA_WIN, 0) // qb

    pieces = []
    zero = jnp.zeros((HEAD_DIM, qb), F32)
    for qt in _group_queries_t(q_ref[0], hpg):
        pieces.append(jnp.concatenate([jnp.where(g == 0, qt, zero), jnp.where(g == 0, zero, qt)], axis=0))
    q_top = jnp.concatenate(pieces, axis=1).astype(BF16)

    kc_hi, kc_lo = _split_bf16(kvc_ref[0, :, 0:kvw])
    s_c = _dot(kc_hi, q_top) + _dot(kc_lo, q_top)
    p_c = _softmax(s_c, (blk + 1) * NSA_BLK - 1 <= qpos, 0, jnp.exp2)
    vct, _ = _value_selector(g, NSA_KV)
    vc_t = _dot_nt(vct, kvc_ref[0, :, kvw:2 * kvw].astype(BF16)).astype(BF16)
    o_cmp = _dot(vc_t, p_c.astype(BF16))[0:HEAD_DIM]
    imp = p_c[:, 0:qb]
    for j in range(1, hpg):
        imp = imp + p_c[:, j * qb:(j + 1) * qb]
    sel = _select_blocks(imp, qpos1 // NSA_BLK, NSA_TOPK - NSA_N_FORCED, 0)
    penalty = ((sel - 1.0) * (-UNSELECTED)).astype(BF16)
    q_slc = jnp.concatenate([q_top, jnp.concatenate([penalty] * hpg, axis=1)], axis=0)

    def rows_of(t):
        return pl.ds(pl.multiple_of(t * qb, qb), qb)

    o_slc = _kmajor_attention(lambda t: kps_ref[rows_of(t), :], q_slc, lambda t: vts_ref[t, g],
                              0, i, lambda t: t * qb + krow <= qpos, False, bufs)

    def win_mask(t):
        d = qpos - (t * qb + krow)
        return (d >= 0) & (d <= NSA_WIN)

    o_win = _kmajor_attention(lambda t: kpw_ref[rows_of(t), :], q_top, lambda t: vtw_ref[t, g],
                              lo_win, i, win_mask, True, bufs)

    def gate_row(branch):
        rows8 = gatet_ref[0, pl.ds(pl.multiple_of(branch * N_HEADS + g * hpg, hpg), hpg), :]
        return jnp.concatenate([rows8[j:j + 1, :] for j in range(hpg)], axis=1)

    out_t = o_cmp * gate_row(0) + o_slc * gate_row(1) + o_win * gate_row(2)
    _store_heads_t(o_ref, out_t, hpg, qb)


def _block_onehot(n_keys, nblk, transposed=False):
    oh = jnp.arange(n_keys)[:, None] // NSA_BLK == jnp.arange(nblk)[None, :]
    return (oh.T if transposed else oh).astype(BF16)


def _nsa_attention(q, gates_t, kvc, kv_bf, batch, seq_len):
    qb = min(256, seq_len)
    nblk = seq_len // NSA_BLK
    cw = kv_bf.shape[1]
    kvw = NSA_KV * HEAD_DIM
    gw = NSA_HPG * HEAD_DIM
    n_tiles = seq_len // qb
    return pl.pallas_call(
        _nsa_kernel, grid=(batch, n_tiles, NSA_KV),
        in_specs=[pl.BlockSpec((1, qb, gw), lambda b, i, g: (b, i, g)),
                  pl.BlockSpec((1, LANES, qb), lambda b, i, g: (b, 0, i)),
                  pl.BlockSpec((1, nblk, kvc.shape[1]), lambda b, i, g: (b, 0, 0)),
                  pl.BlockSpec((1, qb, cw), lambda b, i, g: (b, i, 0)),
                  pl.BlockSpec((qb, nblk), lambda b, i, g: (i, 0))],
        out_specs=pl.BlockSpec((1, 1, qb, gw), lambda b, i, g: (b, g, i, 0)),
        out_shape=jax.ShapeDtypeStruct((batch, NSA_KV, seq_len, gw), BF16),
        scratch_shapes=[pltpu.VMEM((seq_len, kvw + nblk), BF16),
                        pltpu.VMEM((seq_len, kvw), BF16),
                        pltpu.VMEM((n_tiles, NSA_KV, VT_ROWS, qb), BF16),
                        pltpu.VMEM((n_tiles, NSA_KV, VT_ROWS, qb), BF16)]
        + _kmajor_scratch(qb, NSA_HPG * qb),
        compiler_params=_cparams(("parallel", "arbitrary", "arbitrary")), name="nsa_attention")(
            q.reshape(batch, seq_len, q.shape[1]), gates_t, kvc.reshape(batch, nblk, kvc.shape[1]),
            kv_bf.reshape(batch, seq_len, cw), _block_onehot(seq_len, nblk))


N_ROWS = N_HEADS * SUBLANES


def _block_diag_queries(q, n_kv, qbd_ref):
    hpg = N_HEADS // n_kv
    qf = q.astype(F32)
    qbd_ref[...] = jnp.zeros_like(qbd_ref)
    for h in range(N_HEADS):
        g = h // hpg
        qbd_ref[h * SUBLANES:(h + 1) * SUBLANES, g * HEAD_DIM:(g + 1) * HEAD_DIM] = \
            qf[:, h * HEAD_DIM:(h + 1) * HEAD_DIM]
    return qbd_ref[...].astype(BF16)


def _pick_group(res, n_kv):
    hpg = N_HEADS // n_kv
    rg = lax.broadcasted_iota(jnp.int32, (N_ROWS, 1), 0) // (hpg * SUBLANES)
    out = res[:, 0:HEAD_DIM]
    for g in range(1, n_kv):
        out = jnp.where(rg == g, res[:, g * HEAD_DIM:(g + 1) * HEAD_DIM], out)
    return out


def _osm_step_bd(carry, s, mask, v_all, n_kv, v_transposed):
    m, l, acc = carry
    if mask is not None:
        s = jnp.where(mask, s, NEG)
    m_new = jnp.maximum(m, jnp.max(s, axis=-1, keepdims=True))
    p = jnp.exp(s - m_new)
    if mask is not None:
        p = jnp.where(mask, p, 0.0)
    alpha = jnp.exp(m - m_new)
    l = alpha * l + jnp.sum(p, axis=-1, keepdims=True)
    pv = _dot_nt(p.astype(BF16), v_all) if v_transposed else _dot(p.astype(BF16), v_all)
    acc = alpha * acc + _pick_group(pv, n_kv)
    return m_new, l, acc


def _pad_rows(x, rows):
    x = x.astype(F32)
    return jnp.concatenate([x, jnp.zeros((rows - x.shape[0], x.shape[1]), F32)], axis=0)


def _token_of_row():
    return lax.broadcasted_iota(jnp.int32, (N_ROWS, 1), 0) % SUBLANES


def _store_rows(o_ref, out, base):
    for h in range(N_HEADS):
        o_ref[base:base + SUBLANES, h * HEAD_DIM:(h + 1) * HEAD_DIM] = \
            out[h * SUBLANES:(h + 1) * SUBLANES].astype(o_ref.dtype)


def _new_token_step(carry, qbd, kn, vn, n_kv, bias=None):
    k_all = _pad_rows(kn, LANES).astype(BF16)
    v_all = _pad_rows(vn, LANES).astype(BF16)
    s = _dot_nt(qbd, k_all)
    if bias is not None:
        s = s + bias
    lane = lax.broadcasted_iota(jnp.int32, (1, LANES), 1)
    return _osm_step_bd(carry, s, lane <= _token_of_row(), v_all, n_kv, False)


def _window_step(carry, qbd, buf_ref, n_kv, window):
    wb = buf_ref.shape[-1]
    w = n_kv * HEAD_DIM
    lane = lax.broadcasted_iota(jnp.int32, (1, wb), 1)
    kt = buf_ref[0].reshape(w, wb).astype(BF16)
    vt = buf_ref[1].reshape(w, wb).astype(BF16)
    return _osm_step_bd(carry, _dot(qbd, kt), lane >= _token_of_row() + (wb - window), vt, n_kv, True)


def _swa_sample_kernel(sink_ref, q_ref, buf_ref, kvn_ref, o_ref, qbd_ref, *, window):
    n_seq = buf_ref.shape[0]
    w = SWA_KV * HEAD_DIM
    hrow = lax.broadcasted_iota(jnp.int32, (N_ROWS, 1), 0) // SUBLANES
    sink = jnp.zeros((N_ROWS, 1), F32)
    for h in range(N_HEADS):
        sink = jnp.where(hrow == h, sink_ref[h], sink)
    for s_i in range(n_seq):
        rows = slice(s_i * SUBLANES, (s_i + 1) * SUBLANES)
        qbd = _block_diag_queries(q_ref[rows, :], SWA_KV, qbd_ref)
        carry = _window_step(_osm_init(N_ROWS), qbd, buf_ref.at[s_i], SWA_KV, window)
        kvn = kvn_ref[rows, :]
        m, l, acc = _new_token_step(carry, qbd, kvn[:, 0:w], kvn[:, w:2 * w], SWA_KV)
        m_f = jnp.maximum(m, sink)
        scale = jnp.exp(m - m_f)
        out = acc * scale / (l * scale + jnp.exp(sink - m_f))
        _store_rows(o_ref, out, s_i * SUBLANES)


def _swa_sample_attention(q, state_t, layer, kvn_bf, sink):
    n_seq, wb = state_t.shape[1], state_t.shape[-1]
    sb = min(8, n_seq)
    qd = q.shape[1]
    cw = kvn_bf.shape[1]
    assert wb >= SWA_WIN
    return pl.pallas_call(
        functools.partial(_swa_sample_kernel, window=SWA_WIN), grid=(n_seq // sb,),
        in_specs=[pl.BlockSpec(memory_space=pltpu.SMEM),
                  pl.BlockSpec((sb * SUBLANES, qd), lambda i: (i, 0)),
                  pl.BlockSpec((None, sb, 2, SWA_KV, HEAD_DIM, wb), lambda i: (layer, i, 0, 0, 0, 0)),
                  pl.BlockSpec((sb * SUBLANES, cw), lambda i: (i, 0))],
        out_specs=pl.BlockSpec((sb * SUBLANES, qd), lambda i: (i, 0)),
        out_shape=jax.ShapeDtypeStruct(q.shape, BF16),
        scratch_shapes=[pltpu.VMEM((N_ROWS, SWA_KV * HEAD_DIM), F32)],
        compiler_params=_cparams(("parallel",)), name="swa_sample")(sink, q, state_t, kvn_bf)


def _page_specs(n, block, layer, slot_block, n_pages, per_step, last_chunk=None):
    trailing = (0,) * (len(block) - 1)

    def spec(k):
        def index_map(s, c, pt):
            cc = c if last_chunk is None else jnp.minimum(c, last_chunk)
            return (layer, pt[s * n_pages + cc * per_step + k], slot_block) + trailing
        return pl.BlockSpec((None, None) + block, index_map)

    return [spec(k) for k in range(n)]


def _fox_bias_kernel(pt_ref, *refs, ppc):
    page_refs = refs[:ppc]
    lfn_ref, dt_ref, dn_ref, carry_ref = refs[ppc:]
    c = pl.program_id(1)
    page = page_refs[0].shape[1]

    @pl.when(c == 0)
    def _():
        carry_ref[...] = jnp.zeros_like(carry_ref)

    rows = ppc * N_HEADS
    stacked = jnp.concatenate([page_refs[k][...] for k in range(ppc)], axis=0)
    local = _dot(stacked, _tri(page, True), precision=HIGHEST)
    totals = _dot(stacked, jnp.ones((page, page), F32), precision=HIGHEST)
    r = lax.broadcasted_iota(jnp.int32, (rows, rows), 0)
    c_ = lax.broadcasted_iota(jnp.int32, (rows, rows), 1)
    earlier = jnp.where((r % N_HEADS == c_ % N_HEADS) & (c_ < r - r % N_HEADS), 1.0, 0.0).astype(F32)
    d_all = local + _dot(earlier, totals, precision=HIGHEST) + jnp.concatenate([carry_ref[...]] * ppc, axis=0)
    for k in range(ppc):
        dt_ref[0, :, k * page:(k + 1) * page] = d_all[k * N_HEADS:(k + 1) * N_HEADS]
    carry = d_all[rows - N_HEADS:, page - 1:page]
    carry_ref[...] = carry

    @pl.when(c == pl.num_programs(1) - 1)
    def _():
        dn_ref[0] = carry + _dot(lfn_ref[0], _tri(LANES, True), precision=HIGHEST)


def _fox_bias(logf_t, layer, pt_flat, n_seq, n_pages, lf_new_t):
    page = logf_t.shape[3]
    ppc = min(PAGES_PER_STEP, n_pages)
    assert page == LANES and n_pages % ppc == 0
    lfn = jnp.pad(lf_new_t, ((0, 0), (0, 0), (0, LANES - lf_new_t.shape[2])))
    grid_spec = pltpu.PrefetchScalarGridSpec(
        num_scalar_prefetch=1, grid=(n_seq, n_pages // ppc),
        in_specs=_page_specs(ppc, (N_HEADS, page), layer, 0, n_pages, ppc)
        + [pl.BlockSpec((1, N_HEADS, LANES), lambda s, c, pt: (s, 0, 0))],
        out_specs=[pl.BlockSpec((1, N_HEADS, ppc * page), lambda s, c, pt: (s, 0, c)),
                   pl.BlockSpec((1, N_HEADS, LANES), lambda s, c, pt: (s, 0, 0))],
        scratch_shapes=[pltpu.VMEM((N_HEADS, 1), F32)])
    return pl.pallas_call(
        functools.partial(_fox_bias_kernel, ppc=ppc), grid_spec=grid_spec,
        out_shape=[jax.ShapeDtypeStruct((n_seq, N_HEADS, n_pages * page), F32),
                   jax.ShapeDtypeStruct((n_seq, N_HEADS, LANES), F32)],
        compiler_params=_cparams(("parallel", "arbitrary")), name="fox_bias")(
            pt_flat, *([logf_t] * ppc), lfn)


def _rows_from_heads(x):
    return jnp.broadcast_to(x[:, None, :], (N_HEADS, SUBLANES, x.shape[1])).reshape(N_ROWS, x.shape[1])


def _stage_pages(page_refs, kt_ref, vt_ref):
    rows, page = kt_ref.shape[0], page_refs[0].shape[-1]
    for k, ref in enumerate(page_refs):
        kt_ref[:, k * page:(k + 1) * page] = ref[0].reshape(rows, page).astype(BF16)
        vt_ref[:, k * page:(k + 1) * page] = ref[1].reshape(rows, page).astype(BF16)


def _fox_sample_kernel(pt_ref, *refs, ppc):
    page_refs = refs[:ppc]
    (q_ref, dt_ref, dq_ref, dn_ref, kvn_ref, o_ref,
     qbd_ref, m_ref, l_ref, acc_ref, kt_ref, vt_ref) = refs[ppc:]
    c = pl.program_id(1)
    n_chunks = pl.num_programs(1) - 1
    w = FOX_KV * HEAD_DIM

    @pl.when(c == 0)
    def _():
        _block_diag_queries(q_ref[...], FOX_KV, qbd_ref)
        m_ref[...], l_ref[...], acc_ref[...] = _osm_init(N_ROWS)

    qbd = qbd_ref[...].astype(BF16)
    dq = dq_ref[0]

    @pl.when(c < n_chunks)
    def _():
        _stage_pages(page_refs, kt_ref, vt_ref)
        s = _dot(qbd, kt_ref[...]) + (dq - _rows_from_heads(dt_ref[0]))
        carry = _osm_step_bd((m_ref[...], l_ref[...], acc_ref[...]), s, None, vt_ref[...], FOX_KV, True)
        m_ref[...], l_ref[...], acc_ref[...] = carry

    @pl.when(c == n_chunks)
    def _():
        kvn = kvn_ref[...]
        carry = _new_token_step((m_ref[...], l_ref[...], acc_ref[...]), qbd, kvn[:, 0:w], kvn[:, w:2 * w],
                                FOX_KV, bias=dq - _rows_from_heads(dn_ref[0]))
        _store_rows(o_ref, _osm_finish(carry), 0)


def _fox_sample_attention(q, cache_t, layer, pt_flat, n_seq, n_pages, d_past, d_new, kvn_bf):
    page = cache_t.shape[-1]
    w = FOX_KV * HEAD_DIM
    ppc = min(PAGES_PER_STEP, n_pages)
    n_chunks = n_pages // ppc
    qd = q.shape[1]
    dq = d_new[:, :, :SUBLANES].reshape(n_seq, N_ROWS, 1)
    last = n_chunks - 1
    grid_spec = pltpu.PrefetchScalarGridSpec(
        num_scalar_prefetch=1, grid=(n_seq, n_chunks + 1),
        in_specs=_page_specs(ppc, (2, FOX_KV, HEAD_DIM, page), layer, 0, n_pages, ppc, last_chunk=last)
        + [pl.BlockSpec((SUBLANES, qd), lambda s, c, pt: (s, 0)),
           pl.BlockSpec((1, N_HEADS, ppc * page), lambda s, c, pt: (s, 0, jnp.minimum(c, last))),
           pl.BlockSpec((1, N_ROWS, 1), lambda s, c, pt: (s, 0, 0)),
           pl.BlockSpec((1, N_HEADS, LANES), lambda s, c, pt: (s, 0, 0)),
           pl.BlockSpec((SUBLANES, 2 * w), lambda s, c, pt: (s, 0))],
        out_specs=pl.BlockSpec((SUBLANES, qd), lambda s, c, pt: (s, 0)),
        scratch_shapes=[pltpu.VMEM((N_ROWS, w), F32),
                        pltpu.VMEM((N_ROWS, 1), F32), pltpu.VMEM((N_ROWS, 1), F32),
                        pltpu.VMEM((N_ROWS, HEAD_DIM), F32),
                        pltpu.VMEM((w, ppc * page), BF16), pltpu.VMEM((w, ppc * page), BF16)])
    return pl.pallas_call(
        functools.partial(_fox_sample_kernel, ppc=ppc), grid_spec=grid_spec,
        out_shape=jax.ShapeDtypeStruct(q.shape, BF16),
        compiler_params=_cparams(("parallel", "arbitrary")), name="fox_sample")(
            pt_flat, *([cache_t] * ppc), q, d_past, dq, d_new, kvn_bf)


def _compress_paged_kernel(pt_ref, *refs, ppc):
    page_refs = refs[:ppc]
    w_ref, o_ref = refs[ppc:]
    c = pl.program_id(1)
    page = page_refs[0].shape[-1]
    per_page = page // NSA_BLK

    @pl.when(c == 0)
    def _():
        o_ref[...] = jnp.zeros_like(o_ref)

    rows, nblk = o_ref.shape[1], o_ref.shape[2]
    w = w_ref[...].reshape(rows, page)
    prod = jnp.concatenate([page_refs[k][...].reshape(rows, page) * w for k in range(ppc)], axis=1)
    tok_blk = c * ppc * per_page + lax.broadcasted_iota(jnp.int32, (ppc * page, nblk), 0) // NSA_BLK
    indicator = jnp.where(tok_blk == lax.broadcasted_iota(jnp.int32, (ppc * page, nblk), 1), 1.0, 0.0).astype(BF16)
    acc = o_ref[0]
    for part in _split3_bf16(prod):
        acc = acc + _dot(part, indicator)
    o_ref[0] = acc


def _compress_paged(cache_t, layer, pt_flat, n_seq, n_pages, w_cmp):
    page = cache_t.shape[-1]
    per_page = page // NSA_BLK
    ppc = min(PAGES_PER_STEP, n_pages)
    assert n_pages % ppc == 0
    w_t = jnp.tile(jnp.transpose(w_cmp, (1, 2, 3, 0)), (1, 1, 1, per_page))
    blk = (2, NSA_KV, HEAD_DIM, page)
    rows = 2 * NSA_KV * HEAD_DIM
    nblk = n_pages * per_page
    grid_spec = pltpu.PrefetchScalarGridSpec(
        num_scalar_prefetch=1, grid=(n_seq, n_pages // ppc),
        in_specs=_page_specs(ppc, blk, layer, 0, n_pages, ppc)
        + [pl.BlockSpec(blk, lambda s, cc, pt: (0, 0, 0, 0))],
        out_specs=pl.BlockSpec((1, rows, nblk), lambda s, cc, pt: (s, 0, 0)))
    return pl.pallas_call(
        functools.partial(_compress_paged_kernel, ppc=ppc), grid_spec=grid_spec,
        out_shape=jax.ShapeDtypeStruct((n_seq, rows, nblk), F32),
        compiler_params=_cparams(("parallel", "arbitrary")), name="nsa_compress_paged")(
            pt_flat, *([cache_t] * ppc), w_t)


def _nsa_sample_kernel(pt_ref, *refs, ppc, past):
    page_refs = refs[:ppc]
    (q_ref, gate_ref, kvc_ref, onehot_ref, win_ref, kvn_ref, o_ref,
     qbd_ref, sel_ref, ocmp_ref, m_ref, l_ref, acc_ref, kt_ref, vt_ref) = refs[ppc:]
    c = pl.program_id(1)
    n_chunks = pl.num_programs(1) - 1
    w = NSA_KV * HEAD_DIM
    nblk = kvc_ref.shape[-1]
    tok = _token_of_row()

    @pl.when(c == 0)
    def _():
        qbd0 = _block_diag_queries(q_ref[...], NSA_KV, qbd_ref)
        kc_hi, kc_lo = _split_bf16(kvc_ref[0, 0:w])
        s_c = _dot(qbd0, kc_hi) + _dot(qbd0, kc_lo)
        blk = lax.broadcasted_iota(jnp.int32, (1, nblk), 1)
        p_c = _softmax(s_c, (blk + 1) * NSA_BLK - 1 <= past + tok, -1, jnp.exp)
        vct = kvc_ref[0, w:2 * w].astype(BF16)
        ocmp_ref[...] = _pick_group(_dot_nt(p_c.astype(BF16), vct), NSA_KV)
        imp = jnp.sum(p_c.reshape(NSA_KV, NSA_HPG, SUBLANES, nblk), axis=1)
        imp = imp.reshape(NSA_KV * SUBLANES, nblk)
        cur = (past + lax.broadcasted_iota(jnp.int32, (NSA_KV * SUBLANES, 1), 0) % SUBLANES) // NSA_BLK
        sel = _select_blocks(imp, cur, NSA_TOPK - NSA_N_FORCED, 1)
        sel = jnp.broadcast_to(sel.reshape(NSA_KV, 1, SUBLANES, nblk), (NSA_KV, NSA_HPG, SUBLANES, nblk))
        sel_ref[...] = sel.reshape(N_ROWS, nblk)
        m_ref[...], l_ref[...], acc_ref[...] = _osm_init(N_ROWS)

    qbd = qbd_ref[...].astype(BF16)

    @pl.when(c < n_chunks)
    def _():
        _stage_pages(page_refs, kt_ref, vt_ref)
        s = _dot(qbd, kt_ref[...])
        picked = _dot(sel_ref[...].astype(BF16), onehot_ref[...]) > 0.5
        carry = _osm_step_bd((m_ref[...], l_ref[...], acc_ref[...]), s, picked, vt_ref[...], NSA_KV, True)
        m_ref[...], l_ref[...], acc_ref[...] = carry

    @pl.when(c == n_chunks)
    def _():
        kvn = kvn_ref[...]
        o_slc = _osm_finish(_new_token_step((m_ref[...], l_ref[...], acc_ref[...]), qbd,
                                            kvn[:, 0:w], kvn[:, w:2 * w], NSA_KV))
        carry = _window_step(_osm_init(N_ROWS), qbd, win_ref, NSA_KV, NSA_WIN)
        o_win = _osm_finish(_new_token_step(carry, qbd, kvn[:, 2 * w:3 * w], kvn[:, 3 * w:4 * w], NSA_KV))
        gates = gate_ref[...]
        out = (ocmp_ref[...] * _head_columns(gates, 0, N_HEADS)
               + o_slc * _head_columns(gates, N_HEADS, N_HEADS)
               + o_win * _head_columns(gates, 2 * N_HEADS, N_HEADS))
        _store_rows(o_ref, out, 0)


def _nsa_sample_attention(q, gates, kvc_t, cache_t, layer, pt_flat, n_pages, win_t, kvn_bf):
    n_seq, nblk = kvc_t.shape[0], kvc_t.shape[-1]
    page = cache_t.shape[-1]
    past = n_pages * page
    ppc = min(PAGES_PER_STEP, n_pages)
    n_chunks = n_pages // ppc
    last = n_chunks - 1
    qd = q.shape[1]
    wb = win_t.shape[-1]
    w = NSA_KV * HEAD_DIM
    assert past % NSA_BLK == 0 and wb >= NSA_WIN and nblk == past // NSA_BLK and nblk >= NSA_TOPK
    grid_spec = pltpu.PrefetchScalarGridSpec(
        num_scalar_prefetch=1, grid=(n_seq, n_chunks + 1),
        in_specs=_page_specs(ppc, (2, NSA_KV, HEAD_DIM, page), layer, 1, n_pages, ppc, last_chunk=last)
        + [pl.BlockSpec((SUBLANES, qd), lambda s, cc, pt: (s, 0)),
           pl.BlockSpec((SUBLANES, LANES), lambda s, cc, pt: (s, 0)),
           pl.BlockSpec((1, 2 * w, nblk), lambda s, cc, pt: (s, 0, 0)),
           pl.BlockSpec((nblk, ppc * page), lambda s, cc, pt: (0, jnp.minimum(cc, last))),
           pl.BlockSpec((None, None, 2, NSA_KV, HEAD_DIM, wb), lambda s, cc, pt: (layer, s, 0, 0, 0, 0)),
           pl.BlockSpec((SUBLANES, 4 * w), lambda s, cc, pt: (s, 0))],
        out_specs=pl.BlockSpec((SUBLANES, qd), lambda s, cc, pt: (s, 0)),
        scratch_shapes=[pltpu.VMEM((N_ROWS, w), F32),
                        pltpu.VMEM((N_ROWS, nblk), F32),
                        pltpu.VMEM((N_ROWS, HEAD_DIM), F32),
                        pltpu.VMEM((N_ROWS, 1), F32), pltpu.VMEM((N_ROWS, 1), F32),
                        pltpu.VMEM((N_ROWS, HEAD_DIM), F32),
                        pltpu.VMEM((w, ppc * page), BF16), pltpu.VMEM((w, ppc * page), BF16)])
    return pl.pallas_call(
        functools.partial(_nsa_sample_kernel, ppc=ppc, past=past), grid_spec=grid_spec,
        out_shape=jax.ShapeDtypeStruct(q.shape, BF16),
        compiler_params=_cparams(("parallel", "arbitrary")), name="nsa_sample")(
            pt_flat, *([cache_t] * ppc), q, gates, kvc_t, _block_onehot(past, nblk, transposed=True),
            win_t, kvn_bf)


def _pad_cols(w, cols):
    return jnp.pad(w, ((0, 0), (0, cols - w.shape[1])))


def _token_minor(x):
    return jnp.moveaxis(x, 2, -1)


def kernel(x_prompt, x_sample, page_table, cache_nsa_kv, state_nsa_win, state_swa_kv, cache_fox_kv,
           cache_fox_logf, state_ffn_conv, norm_mix, norm_ffn, norm_final, nsa_wq, nsa_wkv, nsa_cmp_w,
           nsa_wg, nsa_bg, nsa_wo, swa_wq, swa_wkv, swa_sink, swa_wo, fox_wq, fox_wkv, fox_wf, fox_bf,
           fox_wo, ffn_w_in, ffn_conv_w, ffn_conv_b, ffn_w_down):
    batch, seq, d = x_prompt.shape
    n_seq, dec, _ = x_sample.shape
    n_pages = page_table.shape[1]
    page = cache_nsa_kv.shape[2]
    past = n_pages * page
    depth = norm_mix.shape[0]
    assert dec == SUBLANES
    xp = x_prompt.reshape(batch * seq, d)
    xs = x_sample.reshape(n_seq * dec, d)
    pos_p = jnp.arange(seq)
    pos_s = past + jnp.arange(dec)
    pt_flat = page_table.reshape(-1)
    nsa_cache_t = _token_minor(cache_nsa_kv)
    nsa_win_t = _token_minor(state_nsa_win)
    swa_state_t = _token_minor(state_swa_kv)
    fox_cache_t = _token_minor(cache_fox_kv)
    fox_logf_t = _token_minor(cache_fox_logf)
    base2 = ATTN_SCALE * LOG2E

    outs = {k: [] for k in ("nsa_kv_p", "nsa_kv_s", "nsa_win_p", "nsa_win_s", "swa_p", "swa_s",
                            "fox_kv_p", "fox_kv_s", "fox_lf_p", "fox_lf_s", "conv_p", "conv_s")}
    for i in range(depth):
        j, kind = i // N_MIXERS, i % N_MIXERS
        g_mix = norm_mix[i]
        if kind == 0:
            wg = _pad_cols(nsa_wg[j], LANES)
            w_cat = jnp.concatenate([nsa_wq[j], nsa_wkv[j], wg], axis=1).astype(BF16)
            extras = {"bg": _pad_cols(nsa_bg[j].reshape(1, -1), LANES), "wg_t": wg.T.astype(BF16)}
            q, kv, win, kvbf, _, gates_t = _project(xp, seq, pos_p, g_mix, w_cat, "nsa", extras, base2)
            kvc = _compress(kv, nsa_cmp_w[j])
            o_p = _nsa_attention(q, _untile_cols(gates_t, batch, seq), kvc, kvbf, batch, seq)
            outs["nsa_kv_p"].append(kv.reshape(batch, seq, 4, NSA_KV, HEAD_DIM))
            outs["nsa_win_p"].append(win.reshape(batch, seq, 2, NSA_KV, HEAD_DIM)[:, seq - min(NSA_WIN, seq):])

            q, kv, win, kvbf, gates, _ = _project(xs, dec, pos_s, g_mix, w_cat, "nsa", extras, ATTN_SCALE)
            kvc_t = _compress_paged(nsa_cache_t, j, pt_flat, n_seq, n_pages, nsa_cmp_w[j])
            o_s = _nsa_sample_attention(q, gates, kvc_t, nsa_cache_t, j, pt_flat, n_pages, nsa_win_t, kvbf)
            outs["nsa_kv_s"].append(kv.reshape(n_seq, dec, 4, NSA_KV, HEAD_DIM))
            win_buf = state_nsa_win[j]
            win_new = win.reshape(n_seq, dec, 2, NSA_KV, HEAD_DIM)
            outs["nsa_win_s"].append(jnp.concatenate([win_buf, win_new], axis=1)[:, -win_buf.shape[1]:])
            wo = nsa_wo[j]
        elif kind == 1:
            w_cat = jnp.concatenate([swa_wq[j], swa_wkv[j]], axis=1).astype(BF16)
            q, kv, kvbf = _project(xp, seq, pos_p, g_mix, w_cat, "swa", None, ATTN_SCALE)
            o_p = _band_attention(q, kvbf, swa_sink[j], batch, seq, SWA_WIN, SWA_KV)
            outs["swa_p"].append(kv.reshape(batch, seq, 2, SWA_KV, HEAD_DIM)[:, seq - min(SWA_WIN, seq):])

            q, kv, kvbf = _project(xs, dec, pos_s, g_mix, w_cat, "swa", None, ATTN_SCALE)
            o_s = _swa_sample_attention(q, swa_state_t, j, kvbf, swa_sink[j])
            buf = state_swa_kv[j]
            kv_new = kv.reshape(n_seq, dec, 2, SWA_KV, HEAD_DIM)
            outs["swa_s"].append(jnp.concatenate([buf, kv_new], axis=1)[:, -buf.shape[1]:])
            wo = swa_wo[j]
        else:
            w_cat = jnp.concatenate([fox_wq[j], fox_wkv[j]], axis=1).astype(BF16)
            extras = {"wf": fox_wf[j], "bf": fox_bf[j]}
            q, kv, kvbf, lf, _ = _project(xp, seq, pos_p, g_mix, w_cat, "fox", extras, base2)
            o_p = _fox_attention(q, kvbf, lf, batch, seq)
            outs["fox_kv_p"].append(kv.reshape(batch, seq, 2, FOX_KV, HEAD_DIM))
            outs["fox_lf_p"].append(lf.reshape(batch, seq, N_HEADS))

            q, kv, kvbf, lf, lft = _project(xs, dec, pos_s, g_mix, w_cat, "fox", extras, ATTN_SCALE)
            d_past, d_new = _fox_bias(fox_logf_t, j, pt_flat, n_seq, n_pages, _untile_cols(lft, n_seq, dec))
            o_s = _fox_sample_attention(q, fox_cache_t, j, pt_flat, n_seq, n_pages, d_past, d_new, kvbf)
            outs["fox_kv_s"].append(kv.reshape(n_seq, dec, 2, FOX_KV, HEAD_DIM))
            outs["fox_lf_s"].append(lf.reshape(n_seq, dec, N_HEADS))
            wo = fox_wo[j]

        wo_bf, w_in_bf, w_down_bf = wo.astype(BF16), ffn_w_in[i].astype(BF16), ffn_w_down[i].astype(BF16)
        xp, conv_p = _post(xp, o_p, seq, wo_bf, norm_ffn[i], w_in_bf, ffn_conv_w[i], ffn_conv_b[i], w_down_bf, None)
        xs, conv_s = _post(xs, o_s, dec, wo_bf, norm_ffn[i], w_in_bf, ffn_conv_w[i], ffn_conv_b[i], w_down_bf,
                           state_ffn_conv[i])
        outs["conv_p"].append(conv_p)
        outs["conv_s"].append(conv_s)

    y_prompt = _final_norm(xp, norm_final).reshape(batch, seq, d)
    y_sample = _final_norm(xs, norm_final).reshape(n_seq, dec, d)
    st = lambda k: jnp.stack(outs[k])
    return (y_prompt, y_sample, st("nsa_kv_p"), st("nsa_kv_s"), st("nsa_win_p"), st("nsa_win_s"),
            st("swa_p"), st("swa_s"), st("fox_kv_p"), st("fox_kv_s"), st("fox_lf_p"), st("fox_lf_s"),
            st("conv_p"), st("conv_s"))
```
